```python
import jax, jax.numpy as jnp
from jax import lax
import numpy as np

D_MODEL = 1024
BATCH = 8
SEQ = 8192
DEPTH = 1
DEC_BATCH = 8
DEC_SEQ = 4096
PAST_LEN = 128

D_MIX = 2 * D_MODEL
D_POOL = D_MIX // 2
D_RET = D_MIX - D_POOL
POOL_WINDOWS = (2, 4, 8, 16)
N_POOL_GROUPS = len(POOL_WINDOWS)
POOL_GROUP = D_POOL // N_POOL_GROUPS
N_RET_HEADS = 8
RET_HEAD_DIM = D_RET // N_RET_HEADS
CHUNK = 128
ROPE_BASE = 10000.0
NORM_EPS = 1e-6
D_IN = D_POOL + 3 * D_RET + D_MIX

kernel_name = "hybrid_pool_retention_encoder"


def rms_norm(x, g):
    xf = x.astype(jnp.float32)
    y = xf * lax.rsqrt(jnp.mean(xf * xf, axis=-1, keepdims=True) + NORM_EPS)
    return (y * g.astype(jnp.float32)).astype(x.dtype)


def multiscale_pool(u, w_grp, scale):
    B, S, _ = u.shape
    uf = u.astype(jnp.float32).reshape(B, S, N_POOL_GROUPS, POOL_GROUP)
    cs = jnp.concatenate([jnp.zeros((B, 1, N_POOL_GROUPS, POOL_GROUP), jnp.float32),
                          jnp.cumsum(uf, axis=1)], axis=1)
    t = np.arange(S)
    outs = []
    for g, w in enumerate(POOL_WINDOWS):
        lo = np.clip(t - w // 2, 0, S)
        hi = np.clip(t + w // 2, 0, S)
        csg = cs[:, :, g]
        win_sum = jnp.take(csg, hi, axis=1) - jnp.take(csg, lo, axis=1)
        mean = win_sum / jnp.asarray(hi - lo, jnp.float32)[None, :, None]
        outs.append(mean - uf[:, :, g])
    p = jnp.stack(outs, axis=2)
    p = jnp.einsum('bsgc,gcd->bsgd', p, w_grp.astype(jnp.float32)).reshape(B, S, D_POOL)
    return p * scale.astype(jnp.float32)


def rope(x, pos):
    half = x.shape[-1] // 2
    freqs = ROPE_BASE ** (-jnp.arange(half, dtype=jnp.float32) / half)
    ang = pos[:, None] * freqs[None, :]
    cos, sin = jnp.cos(ang), jnp.sin(ang)
    x1, x2 = x[..., :half], x[..., half:]
    return jnp.concatenate([x1 * cos - x2 * sin, x2 * cos + x1 * sin], axis=-1)


def log_decay(a):
    return jnp.log1p(-jnp.exp2(-a.astype(jnp.float32)))


def chunk_retention(q, k, v, log_gamma, strict):
    B, H, S, d = q.shape
    n = S // CHUNK
    to_chunks = lambda t: jnp.moveaxis(t.reshape(B, H, n, CHUNK, d), 2, 0)
    idx = jnp.arange(CHUNK, dtype=jnp.float32)
    diff = idx[:, None] - idx[None, :]
    mask = diff > 0 if strict else diff >= 0
    lg = log_gamma[:, None, None]
    D = jnp.where(mask[None], jnp.exp(jnp.where(mask, diff, 0.0)[None] * lg), 0.0)
    q_decay = jnp.exp((idx + 1.0)[None, :] * log_gamma[:, None])[..., None]
    k_decay = jnp.exp((CHUNK - 1.0 - idx)[None, :] * log_gamma[:, None])[..., None]
    chunk_decay = jnp.exp(CHUNK * log_gamma)[:, None, None]

    def step(state, inp):
        qc, kc, vc = inp
        inner = jnp.einsum('bhid,bhjd->bhij', qc, kc) * D
        o = (jnp.einsum('bhij,bhjv->bhiv', inner, vc)
             + jnp.einsum('bhid,bhdv->bhiv', qc * q_decay, state))
        state = state * chunk_decay + jnp.einsum('bhjd,bhjv->bhdv', kc * k_decay, vc)
        return state, o

    state0 = jnp.zeros((B, H, d, d), jnp.float32)
    _, o = lax.scan(step, state0, (to_chunks(q), to_chunks(k), to_chunks(v)))
    return jnp.moveaxis(o, 0, 2).reshape(B, H, S, d)


def bidir_retention(q, k, v, dec_f, dec_b):
    B, S, _ = q.shape
    heads = lambda t: t.reshape(B, S, N_RET_HEADS, RET_HEAD_DIM).transpose(0, 2, 1, 3).astype(jnp.float32)
    pos = jnp.arange(S, dtype=jnp.float32)
    qh = rope(heads(q), pos)
    kh = rope(heads(k), pos) * (RET_HEAD_DIM ** -0.5)
    vh = heads(v)
    rev = lambda t: jnp.flip(t, axis=2)
    o_f = chunk_retention(qh, kh, vh, log_decay(dec_f), strict=False)
    o_b = rev(chunk_retention(rev(qh), rev(kh), rev(vh), log_decay(dec_b), strict=True))
    o = o_f + o_b
    mu = jnp.mean(o, axis=-1, keepdims=True)
    var = jnp.mean(jnp.square(o - mu), axis=-1, keepdims=True)
    o = (o - mu) * lax.rsqrt(var + NORM_EPS)
    return o.transpose(0, 2, 1, 3).reshape(B, S, D_RET)


def encoder_layer(x, c, ada_w, ada_b, g_pre, g_post, w_in, pool_w, pool_scale, dec_f, dec_b, w_out):
    mod = jnp.einsum('bd,de->be', jax.nn.silu(c), ada_w) + ada_b
    shift, scale, gate = jnp.split(mod, 3, axis=-1)
    h = rms_norm(x, g_pre) * (1.0 + scale[:, None, :]) + shift[:, None, :]
    proj = jnp.einsum('bsd,de->bse', h, w_in)
    u_pool, q, k, v, z = jnp.split(
        proj, [D_POOL, D_POOL + D_RET, D_POOL + 2 * D_RET, D_POOL + 3 * D_RET], axis=-1)
    y_pool = multiscale_pool(u_pool, pool_w, pool_scale)
    y_ret = bidir_retention(q, k, v, dec_f, dec_b)
    y = jnp.concatenate([y_pool, y_ret], axis=-1) * jax.nn.silu(z.astype(jnp.float32))
    out = jnp.einsum('bse,ed->bsd', y.astype(x.dtype), w_out)
    return x + gate[:, None, :] * rms_norm(out, g_post)


def setup_inputs(seed: int = 0) -> dict:
    key = jax.random.key(seed)
    ks = jax.random.split(key, 16)
    f32 = jnp.float32
    nrm = lambda k, shape, s: jax.random.normal(k, shape, f32) * s
    base_decay = 5.0 + jnp.arange(N_RET_HEADS, dtype=f32)
    return {
        "x_prompt": nrm(ks[0], (BATCH, SEQ, D_MODEL), 1.0),
        "x_sample": nrm(ks[1], (DEC_BATCH, DEC_SEQ, D_MODEL), 1.0),
        "c_prompt": nrm(ks[2], (BATCH, D_MODEL), 1.0),
        "c_sample": nrm(ks[3], (DEC_BATCH, D_MODEL), 1.0),
        "ada_w": nrm(ks[4], (DEPTH, D_MODEL, 3 * D_MODEL), D_MODEL ** -0.5),
        "ada_b": nrm(ks[5], (DEPTH, 3 * D_MODEL), 0.02),
        "norm_pre": 1.0 + nrm(ks[6], (DEPTH, D_MODEL), 0.02),
        "norm_post": 1.0 + nrm(ks[7], (DEPTH, D_MODEL), 0.02),
        "w_in": nrm(ks[8], (DEPTH, D_MODEL, D_IN), D_MODEL ** -0.5),
        "pool_w": nrm(ks[9], (DEPTH, N_POOL_GROUPS, POOL_GROUP, POOL_GROUP), POOL_GROUP ** -0.5),
        "pool_scale": 1.0 + nrm(ks[10], (DEPTH, D_POOL), 0.02),
        "ret_decay_fwd": base_decay[None, :] + nrm(ks[11], (DEPTH, N_RET_HEADS), 0.1),
        "ret_decay_bwd": base_decay[None, :] + nrm(ks[12], (DEPTH, N_RET_HEADS), 0.1),
        "w_out": nrm(ks[13], (DEPTH, D_MIX, D_MODEL), D_MIX ** -0.5),
    }


def reference(x_prompt, x_sample, c_prompt, c_sample, ada_w, ada_b, norm_pre, norm_post,
              w_in, pool_w, pool_scale, ret_decay_fwd, ret_decay_bwd, w_out):
    y_prompt = x_prompt
    y_sample = x_sample
    for l in range(DEPTH):
        params = (ada_w[l], ada_b[l], norm_pre[l], norm_post[l], w_in[l], pool_w[l],
                  pool_scale[l], ret_decay_fwd[l], ret_decay_bwd[l], w_out[l])
        y_prompt = encoder_layer(y_prompt, c_prompt, *params)
        y_sample = encoder_layer(y_sample, c_sample, *params)
    return (y_prompt, y_sample)
```

```python
import functools

import numpy as np
import jax
import jax.numpy as jnp
from jax import lax
from jax.experimental import pallas as pl
from jax.experimental.pallas import tpu as pltpu

F32 = jnp.float32
BF16 = jnp.bfloat16

POOL_WINDOWS = (2, 4, 8, 16)
N_POOL_GROUPS = len(POOL_WINDOWS)
N_RET_HEADS = 8
CHUNK = 128
ROPE_BASE = 10000.0
NORM_EPS = 1e-6
LANES = 128
POOL_HALO = 64
TOKEN_TILE = 256
VMEM_LIMIT_BYTES = 56 * 1024 * 1024


def _const_spec(shape):
    zeros = (0,) * len(shape)
    return pl.BlockSpec(shape, lambda *_: zeros, pipeline_mode=pl.Buffered(1))


def _sigmoid(x):
    return 1.0 / (1.0 + jnp.exp(-x))


def _rope_table_kernel(freq_ref, cs_ref, sn_ref):
    rows = cs_ref.shape[0]
    base = pl.program_id(0) * rows
    pos = (lax.broadcasted_iota(jnp.int32, (rows, LANES), 0) + base).astype(F32)
    lane = lax.broadcasted_iota(jnp.int32, (rows, LANES), 1)
    ang = pos * freq_ref[...]
    cs_ref[...] = jnp.cos(ang)
    s = jnp.sin(ang)
    sn_ref[...] = jnp.where(lane < LANES // 2, -s, s)


def _rope_tables(seq_len):
    half = LANES // 2
    freqs = ROPE_BASE ** (-jnp.arange(half, dtype=F32) / half)
    freq_row = jnp.concatenate([freqs, freqs]).reshape(1, LANES)
    rows = 512
    return pl.pallas_call(
        _rope_table_kernel,
        grid=(seq_len // rows,),
        in_specs=[pl.BlockSpec((1, LANES), lambda i: (0, 0))],
        out_specs=[pl.BlockSpec((rows, LANES), lambda i: (i, 0))] * 2,
        out_shape=[jax.ShapeDtypeStruct((seq_len, LANES), F32)] * 2,
        name="rope_tables",
    )(freq_row)


def _decay_table_kernel(af_ref, ab_ref, dmask_ref, qf_ref, qb_ref, kf_ref, kb_ref, cd_ref):
    lgf = jnp.log1p(-jnp.exp2(-af_ref[...]))
    lgb = jnp.log1p(-jnp.exp2(-ab_ref[...]))
    width = lgf.shape[1]
    idx = lax.broadcasted_iota(jnp.int32, (CHUNK, width), 0).astype(F32)
    qf_ref[...] = jnp.exp((idx + 1.0) * lgf)
    kf_ref[...] = jnp.exp((CHUNK - 1.0 - idx) * lgf)
    qb_ref[...] = jnp.exp((CHUNK - idx) * lgb)
    kb_ref[...] = jnp.exp(idx * lgb)
    row = lax.broadcasted_iota(jnp.int32, (8, width), 0)
    cd_ref[...] = jnp.where(row == 0, jnp.exp(CHUNK * lgf),
                            jnp.where(row == 1, jnp.exp(CHUNK * lgb), 1.0))
    ii = lax.broadcasted_iota(jnp.int32, (CHUNK, CHUNK), 0)
    jj = lax.broadcasted_iota(jnp.int32, (CHUNK, CHUNK), 1)
    diff = (ii - jj).astype(F32)
    for h in range(N_RET_HEADS):
        cols = slice(h * LANES, (h + 1) * LANES)
        fwd = jnp.exp(jnp.maximum(diff, 0.0) * lgf[:, cols])
        bwd = jnp.exp(jnp.maximum(-diff, 0.0) * lgb[:, cols])
        dmask_ref[h] = jnp.where(diff >= 0, fwd, bwd)


def _decay_tables(dec_f, dec_b):
    width = N_RET_HEADS * LANES
    af = jnp.repeat(dec_f.astype(F32), LANES).reshape(1, width)
    ab = jnp.repeat(dec_b.astype(F32), LANES).reshape(1, width)
    tab = jax.ShapeDtypeStruct((CHUNK, width), F32)
    return pl.pallas_call(
        _decay_table_kernel,
        out_shape=[jax.ShapeDtypeStruct((N_RET_HEADS, CHUNK, CHUNK), F32), tab, tab, tab, tab,
                   jax.ShapeDtypeStruct((8, width), F32)],
        name="decay_tables",
    )(af, ab)


def _adaln_kernel(c_ref, w_ref, b_ref, o_ref):
    c = c_ref[...]
    act = c * _sigmoid(c)
    o_ref[...] = jnp.dot(act, w_ref[...], preferred_element_type=F32,
                         precision=lax.Precision.HIGHEST) + b_ref[...]


def _adaln(c, ada_w, ada_b):
    n, d = c.shape
    e = ada_w.shape[1]
    blk = 1024
    return pl.pallas_call(
        _adaln_kernel,
        grid=(e // blk,),
        in_specs=[pl.BlockSpec((n, d), lambda i: (0, 0)),
                  pl.BlockSpec((d, blk), lambda i: (0, i)),
                  pl.BlockSpec((1, blk), lambda i: (0, i))],
        out_specs=pl.BlockSpec((n, blk), lambda i: (0, i)),
        out_shape=jax.ShapeDtypeStruct((n, e), F32),
        name="adaln",
    )(c, ada_w, ada_b.reshape(1, e))


def _inproj_kernel(x_ref, mod_ref, gpre_ref, w_ref, cs_ref, sn_ref, kb_ref, cd_ref,
                   u_ref, q_ref, k_ref, v_ref, z_ref, sb_ref, state_ref, *, d_pool, d_ret):
    @pl.when(pl.program_id(1) == 0)
    def _():
        state_ref[...] = jnp.zeros_like(state_ref)

    tile = x_ref.shape[1]
    x = x_ref[0]
    ms = jnp.mean(x * x, axis=-1, keepdims=True)
    normed = x * lax.rsqrt(ms + NORM_EPS) * gpre_ref[...]
    h = normed * (1.0 + mod_ref[0, 1:2, :]) + mod_ref[0, 0:1, :]
    hb = h.astype(BF16)

    def proj(lo, width):
        return jnp.dot(hb, w_ref[:, lo:lo + width], preferred_element_type=F32)

    u_ref[0] = proj(0, d_pool).astype(BF16)
    z_ref[0] = proj(d_pool + 3 * d_ret, z_ref.shape[2]).astype(BF16)
    q = proj(d_pool, d_ret)
    k = proj(d_pool + d_ret, d_ret)
    vb = proj(d_pool + 2 * d_ret, d_ret).astype(BF16)
    v_ref[0] = vb

    cs = cs_ref[...]
    sn = sn_ref[...]
    k_scale = float(LANES) ** -0.5
    n_chunks = tile // CHUNK
    for hd in range(N_RET_HEADS):
        cols = slice(hd * LANES, (hd + 1) * LANES)
        qh = q[:, cols]
        q_ref[0, :, cols] = (qh * cs + pltpu.roll(qh, LANES // 2, 1) * sn).astype(BF16)
        kh = k[:, cols]
        kh = (kh * cs + pltpu.roll(kh, LANES // 2, 1) * sn) * k_scale
        k_ref[0, :, cols] = kh.astype(BF16)
        for c in reversed(range(n_chunks)):
            rows = slice(c * CHUNK, (c + 1) * CHUNK)
            st = state_ref[hd]
            sb_ref[0, c, hd] = st.astype(BF16)
            kd = (kh[rows] * kb_ref[:, cols]).astype(BF16)
            upd = lax.dot_general(kd, vb[rows, cols], (((0,), (0,)), ((), ())),
                                  preferred_element_type=F32)
            state_ref[hd] = st * cd_ref[1:2, cols] + upd


def _inproj(x, mod, g_pre, w_in_bf16, cs, sn, kb, cd, d_pool, d_ret, d_mix):
    b, s, d = x.shape
    tile = TOKEN_TILE
    n_tiles = s // tile
    n_chunks = tile // CHUNK
    rev = lambda bi, j: (bi, n_tiles - 1 - j, 0)
    tok_spec = lambda width: pl.BlockSpec((1, tile, width), rev)
    out_shape = [jax.ShapeDtypeStruct((b, s, d_pool), BF16),
                 jax.ShapeDtypeStruct((b, s, d_ret), BF16),
                 jax.ShapeDtypeStruct((b, s, d_ret), BF16),
                 jax.ShapeDtypeStruct((b, s, d_ret), BF16),
                 jax.ShapeDtypeStruct((b, s, d_mix), BF16),
                 jax.ShapeDtypeStruct((b, s // CHUNK, N_RET_HEADS, LANES, LANES), BF16)]
    return pl.pallas_call(
        functools.partial(_inproj_kernel, d_pool=d_pool, d_ret=d_ret),
        grid=(b, n_tiles),
        in_specs=[tok_spec(d),
                  pl.BlockSpec((1, 3, d), lambda bi, j: (bi, 0, 0)),
                  _const_spec((1, d)),
                  _const_spec(w_in_bf16.shape),
                  pl.BlockSpec((tile, LANES), lambda bi, j: (n_tiles - 1 - j, 0)),
                  pl.BlockSpec((tile, LANES), lambda bi, j: (n_tiles - 1 - j, 0)),
                  _const_spec(kb.shape),
                  _const_spec(cd.shape)],
        out_specs=[tok_spec(d_pool), tok_spec(d_ret), tok_spec(d_ret), tok_spec(d_ret),
                   tok_spec(d_mix),
                   pl.BlockSpec((1, n_chunks, N_RET_HEADS, LANES, LANES),
                                lambda bi, j: (bi, n_tiles - 1 - j, 0, 0, 0))],
        out_shape=out_shape,
        scratch_shapes=[pltpu.VMEM((N_RET_HEADS, LANES, LANES), F32)],
        compiler_params=pltpu.CompilerParams(
            dimension_semantics=("arbitrary", "arbitrary"),
            vmem_limit_bytes=VMEM_LIMIT_BYTES),
        name="inproj",
    )(x, mod, g_pre, w_in_bf16, cs, sn, kb, cd)


def _mixer_kernel(x_ref, mod_ref, gpost_ref, u_ref, uprev_ref, unext_ref, q_ref, k_ref, v_ref,
                  z_ref, sb_ref, band_ref, poolw_ref, pscale_ref, dmask_ref, qf_ref, qb_ref,
                  kf_ref, cd_ref, wout_ref, y_ref, state_ref, uext_ref, ybuf_ref, *, seq_len):
    j = pl.program_id(1)
    n_tiles = pl.num_programs(1)
    tile = x_ref.shape[1]
    d_pool = u_ref.shape[2]
    pool_group = d_pool // N_POOL_GROUPS

    @pl.when(j == 0)
    def _():
        state_ref[...] = jnp.zeros_like(state_ref)

    uprev = uprev_ref[0]
    unext = unext_ref[0]
    uext_ref[0:POOL_HALO] = jnp.where(j == 0, jnp.zeros_like(uprev), uprev)
    uext_ref[POOL_HALO:POOL_HALO + tile] = u_ref[0]
    uext_ref[POOL_HALO + tile:] = jnp.where(j == n_tiles - 1, jnp.zeros_like(unext), unext)

    for r in range(tile // CHUNK):
        rows = slice(r * CHUNK, (r + 1) * CHUNK)
        pos = lax.broadcasted_iota(jnp.int32, (CHUNK, 1), 0) + (j * tile + r * CHUNK)
        for g, w in enumerate(POOL_WINDOWS):
            cols = slice(g * pool_group, (g + 1) * pool_group)
            win = uext_ref[r * CHUNK:r * CHUNK + CHUNK + 2 * POOL_HALO, cols]
            wsum = jnp.dot(band_ref[g], win, preferred_element_type=F32)
            cnt = jnp.minimum(pos + w // 2, seq_len) - jnp.maximum(pos - w // 2, 0)
            centre = uext_ref[POOL_HALO + r * CHUNK:POOL_HALO + (r + 1) * CHUNK, cols].astype(F32)
            p = wsum / cnt.astype(F32) - centre
            yp = jnp.dot(p.astype(BF16), poolw_ref[g], preferred_element_type=F32)
            yp = yp * pscale_ref[:, cols]
            zz = z_ref[0, rows, cols].astype(F32)
            ybuf_ref[rows, cols] = (yp * (zz * _sigmoid(zz))).astype(BF16)

    for c in range(tile // CHUNK):
        rows = slice(c * CHUNK, (c + 1) * CHUNK)
        for hd in range(N_RET_HEADS):
            cols = slice(hd * LANES, (hd + 1) * LANES)
            qh = q_ref[0, rows, cols]
            kh = k_ref[0, rows, cols]
            vh = v_ref[0, rows, cols]
            st = state_ref[hd]
            scores = lax.dot_general(qh, kh, (((1,), (1,)), ((), ())), preferred_element_type=F32)
            inner = (scores * dmask_ref[hd]).astype(BF16)
            o = jnp.dot(inner, vh, preferred_element_type=F32)
            states = jnp.concatenate([st.astype(BF16), sb_ref[0, c, hd]], axis=1)
            cross = jnp.dot(qh, states, preferred_element_type=F32)
            o = o + qf_ref[:, cols] * cross[:, :LANES] + qb_ref[:, cols] * cross[:, LANES:]
            mu = jnp.mean(o, axis=-1, keepdims=True)
            dev = o - mu
            var = jnp.mean(dev * dev, axis=-1, keepdims=True)
            on = dev * lax.rsqrt(var + NORM_EPS)
            zz = z_ref[0, rows, d_pool + hd * LANES:d_pool + (hd + 1) * LANES].astype(F32)
            ybuf_ref[rows, d_pool + hd * LANES:d_pool + (hd + 1) * LANES] = (
                on * (zz * _sigmoid(zz))).astype(BF16)
            kd = (kh.astype(F32) * kf_ref[:, cols]).astype(BF16)
            upd = lax.dot_general(kd, vh, (((0,), (0,)), ((), ())), preferred_element_type=F32)
            state_ref[hd] = st * cd_ref[0:1, cols] + upd

    out = jnp.dot(ybuf_ref[...], wout_ref[...], preferred_element_type=F32)
    ms = jnp.mean(out * out, axis=-1, keepdims=True)
    post = out * lax.rsqrt(ms + NORM_EPS) * gpost_ref[...]
    y_ref[0] = x_ref[0] + mod_ref[0, 2:3, :] * post


def _band_matrices():
    i = np.arange(CHUNK)[:, None] + POOL_HALO
    m = np.arange(CHUNK + 2 * POOL_HALO)[None, :]
    mats = [((m >= i - w // 2) & (m < i + w // 2)) for w in POOL_WINDOWS]
    return jnp.asarray(np.stack(mats).astype(np.float32), dtype=BF16)


def _mixer(x, mod, g_post, u, q, k, v, z, sb, band, pool_w_bf16, pool_scale, dmask, qf, qb, kf,
           cd, w_out_bf16):
    b, s, d = x.shape
    tile = TOKEN_TILE
    n_tiles = s // tile
    n_chunks = tile // CHUNK
    d_pool = u.shape[2]
    d_mix = z.shape[2]
    halo_per_tile = tile // POOL_HALO
    n_halo_blocks = s // POOL_HALO
    fwd = lambda bi, j: (bi, j, 0)
    tok_spec = lambda width: pl.BlockSpec((1, tile, width), fwd)
    prev_map = lambda bi, j: (bi, jnp.maximum(j * halo_per_tile - 1, 0), 0)
    next_map = lambda bi, j: (bi, jnp.minimum((j + 1) * halo_per_tile, n_halo_blocks - 1), 0)
    return pl.pallas_call(
        functools.partial(_mixer_kernel, seq_len=s),
        grid=(b, n_tiles),
        in_specs=[tok_spec(d),
                  pl.BlockSpec((1, 3, d), lambda bi, j: (bi, 0, 0)),
                  _const_spec((1, d)),
                  tok_spec(d_pool),
                  pl.BlockSpec((1, POOL_HALO, d_pool), prev_map),
                  pl.BlockSpec((1, POOL_HALO, d_pool), next_map),
                  tok_spec(q.shape[2]), tok_spec(k.shape[2]), tok_spec(v.shape[2]),
                  tok_spec(d_mix),
                  pl.BlockSpec((1, n_chunks, N_RET_HEADS, LANES, LANES),
                               lambda bi, j: (bi, j, 0, 0, 0)),
                  _const_spec(band.shape),
                  _const_spec(pool_w_bf16.shape),
                  _const_spec((1, d_pool)),
                  _const_spec(dmask.shape),
                  _const_spec(qf.shape), _const_spec(qb.shape), _const_spec(kf.shape),
                  _const_spec(cd.shape),
                  _const_spec(w_out_bf16.shape)],
        out_specs=tok_spec(d),
        out_shape=jax.ShapeDtypeStruct((b, s, d), F32),
        scratch_shapes=[pltpu.VMEM((N_RET_HEADS, LANES, LANES), F32),
                        pltpu.VMEM((tile + 2 * POOL_HALO, d_pool), BF16),
                        pltpu.VMEM((tile, d_mix), BF16)],
        compiler_params=pltpu.CompilerParams(
            dimension_semantics=("arbitrary", "arbitrary"),
            vmem_limit_bytes=VMEM_LIMIT_BYTES),
        name="mixer",
    )(x, mod, g_post, u, u, u, q, k, v, z, sb, band, pool_w_bf16, pool_scale, dmask, qf, qb, kf,
      cd, w_out_bf16)


def _layer(xs, cs_list, ada_w, ada_b, g_pre, g_post, w_in, pool_w, pool_scale, dec_f, dec_b, w_out,
           rope_cs, rope_sn):
    d = xs[0].shape[2]
    d_pool = pool_scale.shape[0]
    d_mix = w_out.shape[0]
    d_ret = d_mix - d_pool
    assert d_ret == N_RET_HEADS * LANES and d_pool % (N_POOL_GROUPS * LANES) == 0
    assert w_in.shape[1] == d_pool + 3 * d_ret + d_mix

    dmask, qf, qb, kf, kb, cd = _decay_tables(dec_f, dec_b)
    mod = _adaln(jnp.concatenate(cs_list, axis=0), ada_w, ada_b)
    w_in_b = w_in.astype(BF16)
    w_out_b = w_out.astype(BF16)
    pool_w_b = pool_w.astype(BF16)
    band = _band_matrices()
    g_pre2 = g_pre.reshape(1, d)
    g_post2 = g_post.reshape(1, d)
    pscale2 = pool_scale.reshape(1, d_pool)

    outs = []
    row = 0
    for x in xs:
        b, s, _ = x.shape
        assert s % TOKEN_TILE == 0 and TOKEN_TILE % CHUNK == 0
        m = mod[row:row + b].reshape(b, 3, d)
        row += b
        u, q, k, v, z, sb = _inproj(x, m, g_pre2, w_in_b, rope_cs, rope_sn, kb, cd,
                                    d_pool, d_ret, d_mix)
        outs.append(_mixer(x, m, g_post2, u, q, k, v, z, sb, band, pool_w_b, pscale2, dmask,
                           qf, qb, kf, cd, w_out_b))
    return outs


def kernel(x_prompt, x_sample, c_prompt, c_sample, ada_w, ada_b, norm_pre, norm_post, w_in, pool_w,
           pool_scale, ret_decay_fwd, ret_decay_bwd, w_out):
    xs = [x_prompt, x_sample]
    rope_cs, rope_sn = _rope_tables(max(x.shape[1] for x in xs))
    for l in range(ada_w.shape[0]):
        xs = _layer(xs, [c_prompt, c_sample], ada_w[l], ada_b[l], norm_pre[l], norm_post[l],
                    w_in[l], pool_w[l], pool_scale[l], ret_decay_fwd[l], ret_decay_bwd[l],
                    w_out[l], rope_cs, rope_sn)
    return tuple(xs)
```

```python
import functools

import numpy as np
import jax
import jax.numpy as jnp
from jax import lax
from jax.experimental import pallas as pl
from jax.experimental.pallas import tpu as pltpu

F32 = jnp.float32
BF16 = jnp.bfloat16

POOL_WINDOWS = (2, 4, 8, 16)
N_POOL_GROUPS = len(POOL_WINDOWS)
N_RET_HEADS = 8
CHUNK = 128
ROPE_BASE = 10000.0
NORM_EPS = 1e-6
LANES = 128
POOL_HALO = 64
TOKEN_TILE = 512
OUT_SPLITS = (2, 4)
ITEM_LAG = 1
VMEM_LIMIT_BYTES = 56 * 1024 * 1024


def _const_spec(shape):
    zeros = (0,) * len(shape)
    return pl.BlockSpec(shape, lambda *_: zeros, pipeline_mode=pl.Buffered(1))


def _sigmoid(x):
    return 1.0 / (1.0 + jnp.exp(-x))


def _rope_table_kernel(freq_ref, cs_ref, sn_ref):
    rows = cs_ref.shape[0]
    base = pl.program_id(0) * rows
    pos = (lax.broadcasted_iota(jnp.int32, (rows, LANES), 0) + base).astype(F32)
    lane = lax.broadcasted_iota(jnp.int32, (rows, LANES), 1)
    ang = pos * freq_ref[...]
    cs_ref[...] = jnp.cos(ang)
    s = jnp.sin(ang)
    sn_ref[...] = jnp.where(lane < LANES // 2, -s, s)


def _rope_tables(seq_len):
    half = LANES // 2
    freqs = ROPE_BASE ** (-jnp.arange(half, dtype=F32) / half)
    freq_row = jnp.concatenate([freqs, freqs]).reshape(1, LANES)
    rows = 512
    return pl.pallas_call(
        _rope_table_kernel,
        grid=(seq_len // rows,),
        in_specs=[pl.BlockSpec((1, LANES), lambda i: (0, 0))],
        out_specs=[pl.BlockSpec((rows, LANES), lambda i: (i, 0))] * 2,
        out_shape=[jax.ShapeDtypeStruct((seq_len, LANES), F32)] * 2,
        name="rope_tables",
    )(freq_row)


def _decay_table_kernel(af_ref, ab_ref, dmask_ref, qf_ref, qb_ref, kf_ref, kb_ref, cd_ref):
    lgf = jnp.log1p(-jnp.exp2(-af_ref[...]))
    lgb = jnp.log1p(-jnp.exp2(-ab_ref[...]))
    width = lgf.shape[1]
    idx = lax.broadcasted_iota(jnp.int32, (CHUNK, width), 0).astype(F32)
    qf_ref[...] = jnp.exp((idx + 1.0) * lgf)
    kf_ref[...] = jnp.exp((CHUNK - 1.0 - idx) * lgf)
    qb_ref[...] = jnp.exp((CHUNK - idx) * lgb)
    kb_ref[...] = jnp.exp(idx * lgb)
    row = lax.broadcasted_iota(jnp.int32, (8, width), 0)
    cd_ref[...] = jnp.where(row == 0, jnp.exp(CHUNK * lgf),
                            jnp.where(row == 1, jnp.exp(CHUNK * lgb), 1.0))
    ii = lax.broadcasted_iota(jnp.int32, (CHUNK, CHUNK), 0)
    jj = lax.broadcasted_iota(jnp.int32, (CHUNK, CHUNK), 1)
    diff = (ii - jj).astype(F32)
    for h in range(N_RET_HEADS):
        cols = slice(h * LANES, (h + 1) * LANES)
        fwd = jnp.exp(jnp.maximum(diff, 0.0) * lgf[:, cols])
        bwd = jnp.exp(jnp.maximum(-diff, 0.0) * lgb[:, cols])
        dmask_ref[h] = jnp.where(diff >= 0, fwd, bwd)


def _decay_tables(dec_f, dec_b):
    width = N_RET_HEADS * LANES
    af = jnp.repeat(dec_f.astype(F32), LANES).reshape(1, width)
    ab = jnp.repeat(dec_b.astype(F32), LANES).reshape(1, width)
    tab = jax.ShapeDtypeStruct((CHUNK, width), F32)
    return pl.pallas_call(
        _decay_table_kernel,
        out_shape=[jax.ShapeDtypeStruct((N_RET_HEADS, CHUNK, CHUNK), F32), tab, tab, tab, tab,
                   jax.ShapeDtypeStruct((8, width), F32)],
        name="decay_tables",
    )(af, ab)


def _adaln_kernel(c_ref, w_ref, b_ref, o_ref):
    c = c_ref[...]
    act = c * _sigmoid(c)
    o_ref[...] = jnp.dot(act, w_ref[...], preferred_element_type=F32,
                         precision=lax.Precision.HIGHEST) + b_ref[...]


def _adaln(c, ada_w, ada_b):
    n, d = c.shape
    e = ada_w.shape[1]
    blk = 1024
    return pl.pallas_call(
        _adaln_kernel,
        grid=(e // blk,),
        in_specs=[pl.BlockSpec((n, d), lambda i: (0, 0)),
                  pl.BlockSpec((d, blk), lambda i: (0, i)),
                  pl.BlockSpec((1, blk), lambda i: (0, i))],
        out_specs=pl.BlockSpec((n, blk), lambda i: (0, i)),
        out_shape=jax.ShapeDtypeStruct((n, e), F32),
        name="adaln",
    )(c, ada_w, ada_b.reshape(1, e))


def _inproj_kernel(x_ref, mod_ref, gpre_ref, w_ref, cs_ref, sn_ref, kb_ref, cd_ref,
                   u_ref, q_ref, k_ref, v_ref, z_ref, sb_ref, state_ref, *, d_pool, d_ret):
    @pl.when(pl.program_id(1) == 0)
    def _():
        state_ref[...] = jnp.zeros_like(state_ref)

    tile = x_ref.shape[1]
    x = x_ref[0]
    ms = jnp.mean(x * x, axis=-1, keepdims=True)
    normed = x * lax.rsqrt(ms + NORM_EPS) * gpre_ref[...]
    h = normed * (1.0 + mod_ref[0, 1:2, :]) + mod_ref[0, 0:1, :]
    hb = h.astype(BF16)

    def proj(lo, width):
        return jnp.dot(hb, w_ref[:, lo:lo + width], preferred_element_type=F32)

    u_ref[0] = proj(0, d_pool).astype(BF16)
    z_ref[0] = proj(d_pool + 3 * d_ret, z_ref.shape[2]).astype(BF16)
    q = proj(d_pool, d_ret)
    k = proj(d_pool + d_ret, d_ret)
    vb = proj(d_pool + 2 * d_ret, d_ret).astype(BF16)
    v_ref[0] = vb

    cs = cs_ref[...]
    sn = sn_ref[...]
    k_scale = float(LANES) ** -0.5
    n_chunks = tile // CHUNK
    for hd in range(N_RET_HEADS):
        cols = slice(hd * LANES, (hd + 1) * LANES)
        qh = q[:, cols]
        q_ref[0, :, cols] = (qh * cs + pltpu.roll(qh, LANES // 2, 1) * sn).astype(BF16)
        kh = k[:, cols]
        kh = (kh * cs + pltpu.roll(kh, LANES // 2, 1) * sn) * k_scale
        k_ref[0, :, cols] = kh.astype(BF16)
        for c in reversed(range(n_chunks)):
            rows = slice(c * CHUNK, (c + 1) * CHUNK)
            st = state_ref[hd]
            sb_ref[0, c, hd] = st.astype(BF16)
            kd = (kh[rows] * kb_ref[:, cols]).astype(BF16)
            upd = lax.dot_general(kd, vb[rows, cols], (((0,), (0,)), ((), ())),
                                  preferred_element_type=F32)
            state_ref[hd] = st * cd_ref[1:2, cols] + upd


def _inproj(x, mod, g_pre, w_in_bf16, cs, sn, kb, cd, d_pool, d_ret, d_mix):
    b, s, d = x.shape
    tile = TOKEN_TILE
    n_tiles = s // tile
    n_chunks = tile // CHUNK
    rev = lambda bi, j: (bi, n_tiles - 1 - j, 0)
    tok_spec = lambda width: pl.BlockSpec((1, tile, width), rev)
    out_shape = [jax.ShapeDtypeStruct((b, s, d_pool), BF16),
                 jax.ShapeDtypeStruct((b, s, d_ret), BF16),
                 jax.ShapeDtypeStruct((b, s, d_ret), BF16),
                 jax.ShapeDtypeStruct((b, s, d_ret), BF16),
                 jax.ShapeDtypeStruct((b, s, d_mix), BF16),
                 jax.ShapeDtypeStruct((b, s // CHUNK, N_RET_HEADS, LANES, LANES), BF16)]
    return pl.pallas_call(
        functools.partial(_inproj_kernel, d_pool=d_pool, d_ret=d_ret),
        grid=(b, n_tiles),
        in_specs=[tok_spec(d),
                  pl.BlockSpec((1, 3, d), lambda bi, j: (bi, 0, 0)),
                  _const_spec((1, d)),
                  _const_spec(w_in_bf16.shape),
                  pl.BlockSpec((tile, LANES), lambda bi, j: (n_tiles - 1 - j, 0)),
                  pl.BlockSpec((tile, LANES), lambda bi, j: (n_tiles - 1 - j, 0)),
                  _const_spec(kb.shape),
                  _const_spec(cd.shape)],
        out_specs=[tok_spec(d_pool), tok_spec(d_ret), tok_spec(d_ret), tok_spec(d_ret),
                   tok_spec(d_mix),
                   pl.BlockSpec((1, n_chunks, N_RET_HEADS, LANES, LANES),
                                lambda bi, j: (bi, n_tiles - 1 - j, 0, 0, 0))],
        out_shape=out_shape,
        scratch_shapes=[pltpu.VMEM((N_RET_HEADS, LANES, LANES), F32)],
        compiler_params=pltpu.CompilerParams(
            dimension_semantics=("arbitrary", "arbitrary"),
            vmem_limit_bytes=VMEM_LIMIT_BYTES),
        name="inproj",
    )(x, mod, g_pre, w_in_bf16, cs, sn, kb, cd)


def _mixer_kernel(x_ref, mod_ref, gpost_ref, u_ref, uprev_ref, unext_ref, q_ref, k_ref, v_ref,
                  z_ref, sb_ref, band_ref, poolw_ref, pscale_ref, dmask_ref, qf_ref, qb_ref,
                  kf_ref, cd_ref, wout_ref, y_ref, state_ref, uext_ref, ybuf_ref, *, seq_len):
    j = pl.program_id(1)
    n_tiles = pl.num_programs(1)
    tile = x_ref.shape[1]
    d_pool = u_ref.shape[2]
    pool_group = d_pool // N_POOL_GROUPS

    @pl.when(j == 0)
    def _():
        state_ref[...] = jnp.zeros_like(state_ref)

    uprev = uprev_ref[0]
    unext = unext_ref[0]
    uext_ref[0:POOL_HALO] = jnp.where(j == 0, jnp.zeros_like(uprev), uprev)
    uext_ref[POOL_HALO:POOL_HALO + tile] = u_ref[0]
    uext_ref[POOL_HALO + tile:] = jnp.where(j == n_tiles - 1, jnp.zeros_like(unext), unext)

    head_cols = [slice(hd * LANES, (hd + 1) * LANES) for hd in range(N_RET_HEADS)]
    pool_cols = [slice(g * pool_group, (g + 1) * pool_group) for g in range(N_POOL_GROUPS)]

    def front(c, hd):
        rows = slice(c * CHUNK, (c + 1) * CHUNK)
        cols = head_cols[hd]
        q = q_ref[0, rows, cols]
        k = k_ref[0, rows, cols]
        v = v_ref[0, rows, cols]
        st = state_ref[hd]
        vals = dict(q=q, v=v, st=st.astype(BF16))
        vals["scores"] = lax.dot_general(q, k, (((1,), (1,)), ((), ())), preferred_element_type=F32)
        kd = (k.astype(F32) * kf_ref[:, cols]).astype(BF16)
        upd = lax.dot_general(kd, v, (((0,), (0,)), ((), ())), preferred_element_type=F32)
        state_ref[hd] = st * cd_ref[0:1, cols] + upd
        if hd < N_POOL_GROUPS:
            win = uext_ref[c * CHUNK:c * CHUNK + CHUNK + 2 * POOL_HALO, pool_cols[hd]]
            vals["wsum"] = jnp.dot(band_ref[hd], win, preferred_element_type=F32)
        return vals

    def back(c, hd, vals):
        rows = slice(c * CHUNK, (c + 1) * CHUNK)
        cols = head_cols[hd]
        inner = (vals["scores"] * dmask_ref[hd]).astype(BF16)
        o = jnp.dot(inner, vals["v"], preferred_element_type=F32)
        states = jnp.concatenate([vals["st"], sb_ref[0, c, hd]], axis=1)
        cross = jnp.dot(vals["q"], states, preferred_element_type=F32)
        o = o + qf_ref[:, cols] * cross[:, :LANES] + qb_ref[:, cols] * cross[:, LANES:]
        mu = jnp.mean(o, axis=-1, keepdims=True)
        dev = o - mu
        var = jnp.mean(dev * dev, axis=-1, keepdims=True)
        on = dev * lax.rsqrt(var + NORM_EPS)
        zcols = slice(d_pool + hd * LANES, d_pool + (hd + 1) * LANES)
        zz = z_ref[0, rows, zcols].astype(F32)
        ybuf_ref[rows, zcols] = (on * (zz * _sigmoid(zz))).astype(BF16)
        if hd < N_POOL_GROUPS:
            g, w = hd, POOL_WINDOWS[hd]
            pcols = pool_cols[g]
            pos = lax.broadcasted_iota(jnp.int32, (CHUNK, 1), 0) + (j * tile + c * CHUNK)
            cnt = jnp.minimum(pos + w // 2, seq_len) - jnp.maximum(pos - w // 2, 0)
            centre = uext_ref[POOL_HALO + c * CHUNK:POOL_HALO + (c + 1) * CHUNK, pcols].astype(F32)
            p = vals["wsum"] / cnt.astype(F32) - centre
            yp = jnp.dot(p.astype(BF16), poolw_ref[g], preferred_element_type=F32)
            yp = yp * pscale_ref[:, pcols]
            zp = z_ref[0, rows, pcols].astype(F32)
            ybuf_ref[rows, pcols] = (yp * (zp * _sigmoid(zp))).astype(BF16)

    def output_rows(lo, hi):
        out = jnp.dot(ybuf_ref[lo:hi, :], wout_ref[...], preferred_element_type=F32)
        ms = jnp.mean(out * out, axis=-1, keepdims=True)
        post = out * lax.rsqrt(ms + NORM_EPS) * gpost_ref[...]
        y_ref[0, lo:hi, :] = x_ref[0, lo:hi, :] + mod_ref[0, 2:3, :] * post

    items = [(c, hd) for c in range(tile // CHUNK) for hd in range(N_RET_HEADS)]
    pending = {}
    for step in range(len(items) + ITEM_LAG):
        if step < len(items):
            pending[step] = front(*items[step])
        done = step - ITEM_LAG
        if done >= 0:
            c, hd = items[done]
            back(c, hd, pending.pop(done))
            if hd == N_RET_HEADS - 1 and c + 1 in OUT_SPLITS:
                prev = ([0] + [s for s in OUT_SPLITS if s < c + 1])[-1]
                output_rows(prev * CHUNK, (c + 1) * CHUNK)


def _band_matrices():
    i = np.arange(CHUNK)[:, None] + POOL_HALO
    m = np.arange(CHUNK + 2 * POOL_HALO)[None, :]
    mats = [((m >= i - w // 2) & (m < i + w // 2)) for w in POOL_WINDOWS]
    return jnp.asarray(np.stack(mats).astype(np.float32), dtype=BF16)


def _mixer(x, mod, g_post, u, q, k, v, z, sb, band, pool_w_bf16, pool_scale, dmask, qf, qb, kf,
           cd, w_out_bf16):
    b, s, d = x.shape
    tile = TOKEN_TILE
    n_tiles = s // tile
    n_chunks = tile // CHUNK
    d_pool = u.shape[2]
    d_mix = z.shape[2]
    halo_per_tile = tile // POOL_HALO
    n_halo_blocks = s // POOL_HALO
    fwd = lambda bi, j: (bi, j, 0)
    tok_spec = lambda width: pl.BlockSpec((1, tile, width), fwd)
    prev_map = lambda bi, j: (bi, jnp.maximum(j * halo_per_tile - 1, 0), 0)
    next_map = lambda bi, j: (bi, jnp.minimum((j + 1) * halo_per_tile, n_halo_blocks - 1), 0)
    return pl.pallas_call(
        functools.partial(_mixer_kernel, seq_len=s),
        grid=(b, n_tiles),
        in_specs=[tok_spec(d),
                  pl.BlockSpec((1, 3, d), lambda bi, j: (bi, 0, 0)),
                  _const_spec((1, d)),
                  tok_spec(d_pool),
                  pl.BlockSpec((1, POOL_HALO, d_pool), prev_map),
                  pl.BlockSpec((1, POOL_HALO, d_pool), next_map),
                  tok_spec(q.shape[2]), tok_spec(k.shape[2]), tok_spec(v.shape[2]),
                  tok_spec(d_mix),
                  pl.BlockSpec((1, n_chunks, N_RET_HEADS, LANES, LANES),
                               lambda bi, j: (bi, j, 0, 0, 0)),
                  _const_spec(band.shape),
                  _const_spec(pool_w_bf16.shape),
                  _const_spec((1, d_pool)),
                  _const_spec(dmask.shape),
                  _const_spec(qf.shape), _const_spec(qb.shape), _const_spec(kf.shape),
                  _const_spec(cd.shape),
                  _const_spec(w_out_bf16.shape)],
        out_specs=tok_spec(d),
        out_shape=jax.ShapeDtypeStruct((b, s, d), F32),
        scratch_shapes=[pltpu.VMEM((N_RET_HEADS, LANES, LANES), F32),
                        pltpu.VMEM((tile + 2 * POOL_HALO, d_pool), BF16),
                        pltpu.VMEM((tile, d_mix), BF16)],
        compiler_params=pltpu.CompilerParams(
            dimension_semantics=("arbitrary", "arbitrary"),
            vmem_limit_bytes=VMEM_LIMIT_BYTES),
        name="mixer",
    )(x, mod, g_post, u, u, u, q, k, v, z, sb, band, pool_w_bf16, pool_scale, dmask, qf, qb, kf,
      cd, w_out_bf16)


def _layer(xs, cs_list, ada_w, ada_b, g_pre, g_post, w_in, pool_w, pool_scale, dec_f, dec_b, w_out,
           rope_cs, rope_sn):
    d = xs[0].shape[2]
    d_pool = pool_scale.shape[0]
    d_mix = w_out.shape[0]
    d_ret = d_mix - d_pool
    assert d_ret == N_RET_HEADS * LANES and d_pool % (N_POOL_GROUPS * LANES) == 0
    assert w_in.shape[1] == d_pool + 3 * d_ret + d_mix

    dmask, qf, qb, kf, kb, cd = _decay_tables(dec_f, dec_b)
    mod = _adaln(jnp.concatenate(cs_list, axis=0), ada_w, ada_b)
    w_in_b = w_in.astype(BF16)
    w_out_b = w_out.astype(BF16)
    pool_w_b = pool_w.astype(BF16)
    band = _band_matrices()
    g_pre2 = g_pre.reshape(1, d)
    g_post2 = g_post.reshape(1, d)
    pscale2 = pool_scale.reshape(1, d_pool)

    outs = []
    row = 0
    for x in xs:
        b, s, _ = x.shape
        assert s % TOKEN_TILE == 0 and OUT_SPLITS[-1] * CHUNK == TOKEN_TILE
        m = mod[row:row + b].reshape(b, 3, d)
        row += b
        u, q, k, v, z, sb = _inproj(x, m, g_pre2, w_in_b, rope_cs, rope_sn, kb, cd,
                                    d_pool, d_ret, d_mix)
        outs.append(_mixer(x, m, g_post2, u, q, k, v, z, sb, band, pool_w_b, pscale2, dmask,
                           qf, qb, kf, cd, w_out_b))
    return outs


def kernel(x_prompt, x_sample, c_prompt, c_sample, ada_w, ada_b, norm_pre, norm_post, w_in, pool_w,
           pool_scale, ret_decay_fwd, ret_decay_bwd, w_out):
    xs = [x_prompt, x_sample]
    rope_cs, rope_sn = _rope_tables(max(x.shape[1] for x in xs))
    for l in range(ada_w.shape[0]):
        xs = _layer(xs, [c_prompt, c_sample], ada_w[l], ada_b[l], norm_pre[l], norm_post[l],
                    w_in[l], pool_w[l], pool_scale[l], ret_decay_fwd[l], ret_decay_bwd[l],
                    w_out[l], rope_cs, rope_sn)
    return tuple(xs)
```

```python
import functools

import numpy as np
import jax
import jax.numpy as jnp
from jax import lax
from jax.experimental import pallas as pl
from jax.experimental.pallas import tpu as pltpu

F32 = jnp.float32
BF16 = jnp.bfloat16

POOL_WINDOWS = (2, 4, 8, 16)
N_POOL_GROUPS = len(POOL_WINDOWS)
N_RET_HEADS = 8
CHUNK = 128
ROPE_BASE = 10000.0
NORM_EPS = 1e-6
LANES = 128
POOL_HALO = 64
TOKEN_TILE = 512
PROJ_PIECE = 512
OUT_SPLITS = (2, 4)
ITEM_LAG = 1
VMEM_LIMIT_BYTES = 56 * 1024 * 1024


def _const_spec(shape):
    zeros = (0,) * len(shape)
    return pl.BlockSpec(shape, lambda *_: zeros, pipeline_mode=pl.Buffered(1))


def _sigmoid(x):
    return 1.0 / (1.0 + jnp.exp(-x))


def _pack_rows(x):
    *lead, r, c = x.shape
    pairs = jnp.swapaxes(x.reshape(*lead, r // 2, 2, c), -1, -2)
    return lax.bitcast_convert_type(pairs, jnp.uint32)


def _as_bf16(words):
    return pltpu.bitcast(words, BF16)


def _as_words(x_bf16):
    return pltpu.bitcast(x_bf16, jnp.uint32)


def _rope_table_kernel(freq_ref, cs_ref, sn_ref):
    rows = cs_ref.shape[0]
    base = pl.program_id(0) * rows
    pos = (lax.broadcasted_iota(jnp.int32, (rows, LANES), 0) + base).astype(F32)
    lane = lax.broadcasted_iota(jnp.int32, (rows, LANES), 1)
    ang = pos * freq_ref[...]
    cs_ref[...] = jnp.cos(ang)
    s = jnp.sin(ang)
    sn_ref[...] = jnp.where(lane < LANES // 2, -s, s)


def _rope_tables(seq_len):
    half = LANES // 2
    freqs = ROPE_BASE ** (-jnp.arange(half, dtype=F32) / half)
    freq_row = jnp.concatenate([freqs, freqs]).reshape(1, LANES)
    rows = 512
    return pl.pallas_call(
        _rope_table_kernel,
        grid=(seq_len // rows,),
        in_specs=[pl.BlockSpec((1, LANES), lambda i: (0, 0))],
        out_specs=[pl.BlockSpec((rows, LANES), lambda i: (i, 0))] * 2,
        out_shape=[jax.ShapeDtypeStruct((seq_len, LANES), F32)] * 2,
        name="rope_tables",
    )(freq_row)


def _decay_table_kernel(af_ref, ab_ref, dmask_ref, qf_ref, qb_ref, kf_ref, kb_ref, cd_ref):
    lgf = jnp.log1p(-jnp.exp2(-af_ref[...]))
    lgb = jnp.log1p(-jnp.exp2(-ab_ref[...]))
    width = lgf.shape[1]
    idx = lax.broadcasted_iota(jnp.int32, (CHUNK, width), 0).astype(F32)
    qf_ref[...] = jnp.exp((idx + 1.0) * lgf)
    kf_ref[...] = jnp.exp((CHUNK - 1.0 - idx) * lgf)
    qb_ref[...] = jnp.exp((CHUNK - idx) * lgb)
    kb_ref[...] = jnp.exp(idx * lgb)
    row = lax.broadcasted_iota(jnp.int32, (8, width), 0)
    cd_ref[...] = jnp.where(row == 0, jnp.exp(CHUNK * lgf),
                            jnp.where(row == 1, jnp.exp(CHUNK * lgb), 1.0))
    ii = lax.broadcasted_iota(jnp.int32, (CHUNK, CHUNK), 0)
    jj = lax.broadcasted_iota(jnp.int32, (CHUNK, CHUNK), 1)
    diff = (ii - jj).astype(F32)
    for h in range(N_RET_HEADS):
        cols = slice(h * LANES, (h + 1) * LANES)
        fwd = jnp.exp(jnp.maximum(diff, 0.0) * lgf[:, cols])
        bwd = jnp.exp(jnp.maximum(-diff, 0.0) * lgb[:, cols])
        dmask_ref[h] = jnp.where(diff >= 0, fwd, bwd)


def _decay_tables(dec_f, dec_b):
    width = N_RET_HEADS * LANES
    af = jnp.repeat(dec_f.astype(F32), LANES).reshape(1, width)
    ab = jnp.repeat(dec_b.astype(F32), LANES).reshape(1, width)
    tab = jax.ShapeDtypeStruct((CHUNK, width), F32)
    return pl.pallas_call(
        _decay_table_kernel,
        out_shape=[jax.ShapeDtypeStruct((N_RET_HEADS, CHUNK, CHUNK), F32), tab, tab, tab, tab,
                   jax.ShapeDtypeStruct((8, width), F32)],
        name="decay_tables",
    )(af, ab)


def _adaln_kernel(c_ref, w_ref, b_ref, o_ref):
    c = c_ref[...]
    act = c * _sigmoid(c)
    o_ref[...] = jnp.dot(act, w_ref[...], preferred_element_type=F32,
                         precision=lax.Precision.HIGHEST) + b_ref[...]


def _adaln(c, ada_w, ada_b):
    n, d = c.shape
    e = ada_w.shape[1]
    blk = 1024
    return pl.pallas_call(
        _adaln_kernel,
        grid=(e // blk,),
        in_specs=[pl.BlockSpec((n, d), lambda i: (0, 0)),
                  pl.BlockSpec((d, blk), lambda i: (0, i)),
                  pl.BlockSpec((1, blk), lambda i: (0, i))],
        out_specs=pl.BlockSpec((n, blk), lambda i: (0, i)),
        out_shape=jax.ShapeDtypeStruct((n, e), F32),
        name="adaln",
    )(c, ada_w, ada_b.reshape(1, e))


def _inproj_kernel(x_ref, mod_ref, gpre_ref, w_ref, cs_ref, sn_ref, kb_ref, cd_ref, pscale_ref,
                   u_ref, q_ref, k_ref, v_ref, g_ref, sb_ref, state_ref, *, d_pool, d_ret):
    @pl.when(pl.program_id(1) == 0)
    def _():
        state_ref[...] = jnp.zeros_like(state_ref)

    tile = x_ref.shape[1]
    x = x_ref[0]
    ms = jnp.mean(x * x, axis=-1, keepdims=True)
    normed = x * lax.rsqrt(ms + NORM_EPS) * gpre_ref[...]
    h = normed * (1.0 + mod_ref[0, 1:2, :]) + mod_ref[0, 0:1, :]
    hb = h.astype(BF16)

    def proj(lo, width):
        return jnp.dot(hb, _as_bf16(w_ref[:, lo:lo + width]), preferred_element_type=F32)

    def gate_piece(lo):
        z = proj(d_pool + 3 * d_ret + lo, PROJ_PIECE)
        half_z = 0.5 * z
        gate = half_z * (1.0 + jnp.tanh(half_z))
        if lo < d_pool:
            gate = gate * pscale_ref[:, lo:lo + PROJ_PIECE]
        g_ref[0, :, lo:lo + PROJ_PIECE] = _as_words(gate.astype(BF16))

    pieces = range(0, d_ret, PROJ_PIECE)
    k_parts, v_parts, q_parts = [], [], []
    gate_los = iter(range(0, g_ref.shape[2], PROJ_PIECE))
    for lo in pieces:
        gate_piece(next(gate_los))
        k_parts.append(proj(d_pool + d_ret + lo, PROJ_PIECE))
    for lo in pieces:
        gate_piece(next(gate_los))
        vp = proj(d_pool + 2 * d_ret + lo, PROJ_PIECE).astype(BF16)
        v_ref[0, :, lo:lo + PROJ_PIECE] = _as_words(vp)
        v_parts.append(vp)
    for lo in pieces:
        q_parts.append(proj(d_pool + lo, PROJ_PIECE))
    for lo in range(0, d_pool, PROJ_PIECE):
        u_ref[0, :, lo:lo + PROJ_PIECE] = _as_words(proj(lo, PROJ_PIECE).astype(BF16))

    cs = cs_ref[...]
    sn = sn_ref[...]
    k_scale = float(LANES) ** -0.5
    n_chunks = tile // CHUNK
    heads_per_piece = PROJ_PIECE // LANES
    for hd in range(N_RET_HEADS):
        cols = slice(hd * LANES, (hd + 1) * LANES)
        part, local = hd // heads_per_piece, hd % heads_per_piece
        pcols = slice(local * LANES, (local + 1) * LANES)
        qh = q_parts[part][:, pcols]
        q_ref[0, :, cols] = _as_words((qh * cs + pltpu.roll(qh, LANES // 2, 1) * sn).astype(BF16))
        kh = k_parts[part][:, pcols]
        kh = (kh * cs + pltpu.roll(kh, LANES // 2, 1) * sn) * k_scale
        k_ref[0, :, cols] = _as_words(kh.astype(BF16))
        vh = v_parts[part][:, pcols]
        for c in reversed(range(n_chunks)):
            rows = slice(c * CHUNK, (c + 1) * CHUNK)
            st = state_ref[hd]
            sb_ref[0, c, hd] = _as_words(st.astype(BF16))
            kd = (kh[rows] * kb_ref[:, cols]).astype(BF16)
            upd = lax.dot_general(kd, vh[rows], (((0,), (0,)), ((), ())),
                                  preferred_element_type=F32)
            state_ref[hd] = st * cd_ref[1:2, cols] + upd


def _inproj(x, mod, g_pre, w_in_words, cs, sn, kb, cd, pool_scale, d_pool, d_ret, d_mix):
    b, s, d = x.shape
    tile = TOKEN_TILE
    n_tiles = s // tile
    n_chunks = tile // CHUNK
    rev = lambda bi, j: (bi, n_tiles - 1 - j, 0)
    word_spec = lambda width: pl.BlockSpec((1, tile // 2, width), rev)
    words = lambda width: jax.ShapeDtypeStruct((b, s // 2, width), jnp.uint32)
    out_shape = [words(d_pool), words(d_ret), words(d_ret), words(d_ret), words(d_mix),
                 jax.ShapeDtypeStruct((b, s // CHUNK, N_RET_HEADS, LANES // 2, LANES), jnp.uint32)]
    return pl.pallas_call(
        functools.partial(_inproj_kernel, d_pool=d_pool, d_ret=d_ret),
        grid=(b, n_tiles),
        in_specs=[pl.BlockSpec((1, tile, d), rev),
                  pl.BlockSpec((1, 3, d), lambda bi, j: (bi, 0, 0)),
                  _const_spec((1, d)),
                  _const_spec(w_in_words.shape),
                  pl.BlockSpec((tile, LANES), lambda bi, j: (n_tiles - 1 - j, 0)),
                  pl.BlockSpec((tile, LANES), lambda bi, j: (n_tiles - 1 - j, 0)),
                  _const_spec(kb.shape),
                  _const_spec(cd.shape),
                  _const_spec((1, d_pool))],
        out_specs=[word_spec(d_pool), word_spec(d_ret), word_spec(d_ret), word_spec(d_ret),
                   word_spec(d_mix),
                   pl.BlockSpec((1, n_chunks, N_RET_HEADS, LANES // 2, LANES),
                                lambda bi, j: (bi, n_tiles - 1 - j, 0, 0, 0))],
        out_shape=out_shape,
        scratch_shapes=[pltpu.VMEM((N_RET_HEADS, LANES, LANES), F32)],
        compiler_params=pltpu.CompilerParams(
            dimension_semantics=("arbitrary", "arbitrary"),
            vmem_limit_bytes=VMEM_LIMIT_BYTES),
        name="inproj",
    )(x, mod, g_pre, w_in_words, cs, sn, kb, cd, pool_scale)


def _mixer_kernel(x_ref, mod_ref, gpost_ref, u_ref, uprev_ref, unext_ref, q_ref, k_ref, v_ref,
                  g_ref, sb_ref, band_ref, poolw_ref, dmask_ref, qf_ref, qb_ref,
                  kf_ref, cd_ref, wout_ref, y_ref, state_ref, uext_ref, ybuf_ref, *, seq_len):
    j = pl.program_id(1)
    n_tiles = pl.num_programs(1)
    tile = x_ref.shape[1]
    d_pool = u_ref.shape[2]
    pool_group = d_pool // N_POOL_GROUPS
    half_chunk = CHUNK // 2
    half_halo = POOL_HALO // 2

    @pl.when(j == 0)
    def _():
        state_ref[...] = jnp.zeros_like(state_ref)

    uprev = uprev_ref[0]
    unext = unext_ref[0]
    uext_ref[0:half_halo] = jnp.where(j == 0, jnp.zeros_like(uprev), uprev)
    uext_ref[half_halo:half_halo + tile // 2] = u_ref[0]
    uext_ref[half_halo + tile // 2:] = jnp.where(j == n_tiles - 1, jnp.zeros_like(unext), unext)

    head_cols = [slice(hd * LANES, (hd + 1) * LANES) for hd in range(N_RET_HEADS)]
    pool_cols = [slice(g * pool_group, (g + 1) * pool_group) for g in range(N_POOL_GROUPS)]

    def front(c, hd):
        wrows = slice(c * half_chunk, (c + 1) * half_chunk)
        cols = head_cols[hd]
        q = _as_bf16(q_ref[0, wrows, cols])
        k = _as_bf16(k_ref[0, wrows, cols])
        v = _as_bf16(v_ref[0, wrows, cols])
        st = state_ref[hd]
        vals = dict(q=q, v=v, st=st.astype(BF16))
        vals["scores"] = lax.dot_general(q, k, (((1,), (1,)), ((), ())), preferred_element_type=F32)
        kd = (k.astype(F32) * kf_ref[:, cols]).astype(BF16)
        upd = lax.dot_general(kd, v, (((0,), (0,)), ((), ())), preferred_element_type=F32)
        state_ref[hd] = st * cd_ref[0:1, cols] + upd
        if hd < N_POOL_GROUPS:
            win = _as_bf16(uext_ref[c * half_chunk:(c + 1) * half_chunk + 2 * half_halo, pool_cols[hd]])
            vals["wsum"] = jnp.dot(_as_bf16(band_ref[hd]), win, preferred_element_type=F32)
        return vals

    def back(c, hd, vals):
        rows = slice(c * CHUNK, (c + 1) * CHUNK)
        wrows = slice(c * half_chunk, (c + 1) * half_chunk)
        cols = head_cols[hd]
        inner = (vals["scores"] * dmask_ref[hd]).astype(BF16)
        o = jnp.dot(inner, vals["v"], preferred_element_type=F32)
        states = jnp.concatenate([vals["st"], _as_bf16(sb_ref[0, c, hd])], axis=1)
        cross = jnp.dot(vals["q"], states, preferred_element_type=F32)
        o = o + qf_ref[:, cols] * cross[:, :LANES] + qb_ref[:, cols] * cross[:, LANES:]
        mu = jnp.mean(o, axis=-1, keepdims=True)
        dev = o - mu
        var = jnp.mean(dev * dev, axis=-1, keepdims=True)
        on = dev * lax.rsqrt(var + NORM_EPS)
        zcols = slice(d_pool + hd * LANES, d_pool + (hd + 1) * LANES)
        gate = _as_bf16(g_ref[0, wrows, zcols]).astype(F32)
        ybuf_ref[rows, zcols] = (on * gate).astype(BF16)
        if hd < N_POOL_GROUPS:
            g, w = hd, POOL_WINDOWS[hd]
            pcols = pool_cols[g]
            pos = lax.broadcasted_iota(jnp.int32, (CHUNK, 1), 0) + (j * tile + c * CHUNK)
            cnt = jnp.minimum(pos + w // 2, seq_len) - jnp.maximum(pos - w // 2, 0)
            centre = _as_bf16(uext_ref[half_halo + c * half_chunk:half_halo + (c + 1) * half_chunk,
                                       pcols]).astype(F32)
            p = vals["wsum"] / cnt.astype(F32) - centre
            yp = jnp.dot(p.astype(BF16), _as_bf16(poolw_ref[g]), preferred_element_type=F32)
            pgate = _as_bf16(g_ref[0, wrows, pcols]).astype(F32)
            ybuf_ref[rows, pcols] = (yp * pgate).astype(BF16)

    def output_rows(lo, hi):
        out = jnp.dot(ybuf_ref[lo:hi, :], _as_bf16(wout_ref[...]), preferred_element_type=F32)
        ms = jnp.mean(out * out, axis=-1, keepdims=True)
        post = out * lax.rsqrt(ms + NORM_EPS) * gpost_ref[...]
        y_ref[0, lo:hi, :] = x_ref[0, lo:hi, :] + mod_ref[0, 2:3, :] * post

    items = [(c, hd) for c in range(tile // CHUNK) for hd in range(N_RET_HEADS)]
    pending = {}
    for step in range(len(items) + ITEM_LAG):
        if step < len(items):
            pending[step] = front(*items[step])
        done = step - ITEM_LAG
        if done >= 0:
            c, hd = items[done]
            back(c, hd, pending.pop(done))
            if hd == N_RET_HEADS - 1 and c + 1 in OUT_SPLITS:
                prev = ([0] + [s for s in OUT_SPLITS if s < c + 1])[-1]
                output_rows(prev * CHUNK, (c + 1) * CHUNK)


def _band_matrices():
    i = np.arange(CHUNK)[:, None] + POOL_HALO
    m = np.arange(CHUNK + 2 * POOL_HALO)[None, :]
    mats = [((m >= i - w // 2) & (m < i + w // 2)) for w in POOL_WINDOWS]
    return jnp.asarray(np.stack(mats).astype(np.float32), dtype=BF16)


def _mixer(x, mod, g_post, u, q, k, v, g, sb, band, pool_w_words, dmask, qf, qb, kf, cd,
           w_out_words):
    b, s, d = x.shape
    tile = TOKEN_TILE
    n_tiles = s // tile
    n_chunks = tile // CHUNK
    d_pool = u.shape[2]
    d_mix = g.shape[2]
    halo_per_tile = tile // POOL_HALO
    n_halo_blocks = s // POOL_HALO
    fwd = lambda bi, j: (bi, j, 0)
    word_spec = lambda width: pl.BlockSpec((1, tile // 2, width), fwd)
    prev_map = lambda bi, j: (bi, jnp.maximum(j * halo_per_tile - 1, 0), 0)
    next_map = lambda bi, j: (bi, jnp.minimum((j + 1) * halo_per_tile, n_halo_blocks - 1), 0)
    return pl.pallas_call(
        functools.partial(_mixer_kernel, seq_len=s),
        grid=(b, n_tiles),
        in_specs=[pl.BlockSpec((1, tile, d), fwd),
                  pl.BlockSpec((1, 3, d), lambda bi, j: (bi, 0, 0)),
                  _const_spec((1, d)),
                  word_spec(d_pool),
                  pl.BlockSpec((1, POOL_HALO // 2, d_pool), prev_map),
                  pl.BlockSpec((1, POOL_HALO // 2, d_pool), next_map),
                  word_spec(q.shape[2]), word_spec(k.shape[2]), word_spec(v.shape[2]),
                  word_spec(d_mix),
                  pl.BlockSpec((1, n_chunks, N_RET_HEADS, LANES // 2, LANES),
                               lambda bi, j: (bi, j, 0, 0, 0)),
                  _const_spec(band.shape),
                  _const_spec(pool_w_words.shape),
                  _const_spec(dmask.shape),
                  _const_spec(qf.shape), _const_spec(qb.shape), _const_spec(kf.shape),
                  _const_spec(cd.shape),
                  _const_spec(w_out_words.shape)],
        out_specs=pl.BlockSpec((1, tile, d), fwd),
        out_shape=jax.ShapeDtypeStruct((b, s, d), F32),
        scratch_shapes=[pltpu.VMEM((N_RET_HEADS, LANES, LANES), F32),
                        pltpu.VMEM(((tile + 2 * POOL_HALO) // 2, d_pool), jnp.uint32),
                        pltpu.VMEM((tile, d_mix), BF16)],
        compiler_params=pltpu.CompilerParams(
            dimension_semantics=("arbitrary", "arbitrary"),
            vmem_limit_bytes=VMEM_LIMIT_BYTES),
        name="mixer",
    )(x, mod, g_post, u, u, u, q, k, v, g, sb, band, pool_w_words, dmask, qf, qb, kf, cd,
      w_out_words)


def _layer(xs, cs_list, ada_w, ada_b, g_pre, g_post, w_in, pool_w, pool_scale, dec_f, dec_b, w_out,
           rope_cs, rope_sn):
    d = xs[0].shape[2]
    d_pool = pool_scale.shape[0]
    d_mix = w_out.shape[0]
    d_ret = d_mix - d_pool
    assert d_ret == N_RET_HEADS * LANES and d_pool % (N_POOL_GROUPS * LANES) == 0
    assert w_in.shape[1] == d_pool + 3 * d_ret + d_mix

    dmask, qf, qb, kf, kb, cd = _decay_tables(dec_f, dec_b)
    mod = _adaln(jnp.concatenate(cs_list, axis=0), ada_w, ada_b)
    w_in_words = _pack_rows(w_in.astype(BF16))
    w_out_words = _pack_rows(w_out.astype(BF16))
    pool_w_words = _pack_rows(pool_w.astype(BF16))
    band = _pack_rows(_band_matrices())
    g_pre2 = g_pre.reshape(1, d)
    g_post2 = g_post.reshape(1, d)
    pscale2 = pool_scale.reshape(1, d_pool)

    outs = []
    row = 0
    for x in xs:
        b, s, _ = x.shape
        assert s % TOKEN_TILE == 0 and OUT_SPLITS[-1] * CHUNK == TOKEN_TILE
        m = mod[row:row + b].reshape(b, 3, d)
        row += b
        u, q, k, v, g, sb = _inproj(x, m, g_pre2, w_in_words, rope_cs, rope_sn, kb, cd, pscale2,
                                    d_pool, d_ret, d_mix)
        outs.append(_mixer(x, m, g_post2, u, q, k, v, g, sb, band, pool_w_words, dmask,
                           qf, qb, kf, cd, w_out_words))
    return outs


def kernel(x_prompt, x_sample, c_prompt, c_sample, ada_w, ada_b, norm_pre, norm_post, w_in, pool_w,
           pool_scale, ret_decay_fwd, ret_decay_bwd, w_out):
    xs = [x_prompt, x_sample]
    rope_cs, rope_sn = _rope_tables(max(x.shape[1] for x in xs))
    for l in range(ada_w.shape[0]):
        xs = _layer(xs, [c_prompt, c_sample], ada_w[l], ada_b[l], norm_pre[l], norm_post[l],
                    w_in[l], pool_w[l], pool_scale[l], ret_decay_fwd[l], ret_decay_bwd[l],
                    w_out[l], rope_cs, rope_sn)
    return tuple(xs)
```

```python
import functools

import numpy as np
import jax
import jax.numpy as jnp
from jax import lax
from jax.experimental import pallas as pl
from jax.experimental.pallas import tpu as pltpu

F32 = jnp.float32
BF16 = jnp.bfloat16

POOL_WINDOWS = (2, 4, 8, 16)
N_POOL_GROUPS = len(POOL_WINDOWS)
N_RET_HEADS = 8
CHUNK = 128
ROPE_BASE = 10000.0
NORM_EPS = 1e-6
LANES = 128
POOL_HALO = 64
TOKEN_TILE = 512
PROJ_PIECE = 512
OUT_SPLITS = (2, 4)
ITEM_LAG = 1
VMEM_LIMIT_BYTES = 56 * 1024 * 1024


def _const_spec(shape):
    zeros = (0,) * len(shape)
    return pl.BlockSpec(shape, lambda *_: zeros, pipeline_mode=pl.Buffered(1))


def _sigmoid(x):
    return 1.0 / (1.0 + jnp.exp(-x))


def _col_blocks(x):
    *lead, r, c = x.shape
    return jnp.swapaxes(x.reshape(*lead, r, c // LANES, LANES), -2, -3)


def _cat_blocks(ref, idx, blocks, rows=slice(None)):
    return jnp.concatenate([ref[idx + (blk, rows, slice(None))] for blk in blocks], axis=1)


def _rope_table_kernel(freq_ref, cs_ref, sn_ref):
    rows = cs_ref.shape[0]
    base = pl.program_id(0) * rows
    pos = (lax.broadcasted_iota(jnp.int32, (rows, LANES), 0) + base).astype(F32)
    lane = lax.broadcasted_iota(jnp.int32, (rows, LANES), 1)
    ang = pos * freq_ref[...]
    cs_ref[...] = jnp.cos(ang)
    s = jnp.sin(ang)
    sn_ref[...] = jnp.where(lane < LANES // 2, -s, s)


def _rope_tables(seq_len):
    half = LANES // 2
    freqs = ROPE_BASE ** (-jnp.arange(half, dtype=F32) / half)
    freq_row = jnp.concatenate([freqs, freqs]).reshape(1, LANES)
    rows = 512
    return pl.pallas_call(
        _rope_table_kernel,
        grid=(seq_len // rows,),
        in_specs=[pl.BlockSpec((1, LANES), lambda i: (0, 0))],
        out_specs=[pl.BlockSpec((rows, LANES), lambda i: (i, 0))] * 2,
        out_shape=[jax.ShapeDtypeStruct((seq_len, LANES), F32)] * 2,
        name="rope_tables",
    )(freq_row)


def _decay_table_kernel(af_ref, ab_ref, dmask_ref, qf_ref, qb_ref, kf_ref, kb_ref, cd_ref):
    lgf = jnp.log1p(-jnp.exp2(-af_ref[...]))
    lgb = jnp.log1p(-jnp.exp2(-ab_ref[...]))
    width = lgf.shape[1]
    idx = lax.broadcasted_iota(jnp.int32, (CHUNK, width), 0).astype(F32)
    qf_ref[...] = jnp.exp((idx + 1.0) * lgf)
    kf_ref[...] = jnp.exp((CHUNK - 1.0 - idx) * lgf)
    qb_ref[...] = jnp.exp((CHUNK - idx) * lgb)
    kb_ref[...] = jnp.exp(idx * lgb)
    row = lax.broadcasted_iota(jnp.int32, (8, width), 0)
    cd_ref[...] = jnp.where(row == 0, jnp.exp(CHUNK * lgf),
                            jnp.where(row == 1, jnp.exp(CHUNK * lgb), 1.0))
    ii = lax.broadcasted_iota(jnp.int32, (CHUNK, CHUNK), 0)
    jj = lax.broadcasted_iota(jnp.int32, (CHUNK, CHUNK), 1)
    diff = (ii - jj).astype(F32)
    for h in range(N_RET_HEADS):
        cols = slice(h * LANES, (h + 1) * LANES)
        fwd = jnp.exp(jnp.maximum(diff, 0.0) * lgf[:, cols])
        bwd = jnp.exp(jnp.maximum(-diff, 0.0) * lgb[:, cols])
        dmask_ref[h] = jnp.where(diff >= 0, fwd, bwd)


def _decay_tables(dec_f, dec_b):
    width = N_RET_HEADS * LANES
    af = jnp.repeat(dec_f.astype(F32), LANES).reshape(1, width)
    ab = jnp.repeat(dec_b.astype(F32), LANES).reshape(1, width)
    tab = jax.ShapeDtypeStruct((CHUNK, width), F32)
    return pl.pallas_call(
        _decay_table_kernel,
        out_shape=[jax.ShapeDtypeStruct((N_RET_HEADS, CHUNK, CHUNK), F32), tab, tab, tab, tab,
                   jax.ShapeDtypeStruct((8, width), F32)],
        name="decay_tables",
    )(af, ab)


def _adaln_kernel(c_ref, w_ref, b_ref, o_ref):
    c = c_ref[...]
    act = c * _sigmoid(c)
    o_ref[...] = jnp.dot(act, w_ref[...], preferred_element_type=F32,
                         precision=lax.Precision.HIGHEST) + b_ref[...]


def _adaln(c, ada_w, ada_b):
    n, d = c.shape
    e = ada_w.shape[1]
    blk = 1024
    return pl.pallas_call(
        _adaln_kernel,
        grid=(e // blk,),
        in_specs=[pl.BlockSpec((n, d), lambda i: (0, 0)),
                  pl.BlockSpec((d, blk), lambda i: (0, i)),
                  pl.BlockSpec((1, blk), lambda i: (0, i))],
        out_specs=pl.BlockSpec((n, blk), lambda i: (0, i)),
        out_shape=jax.ShapeDtypeStruct((n, e), F32),
        name="adaln",
    )(c, ada_w, ada_b.reshape(1, e))


def _inproj_kernel(x_ref, mod_ref, gpre_ref, w_ref, cs_ref, sn_ref, kb_ref, cd_ref, pscale_ref,
                   u_ref, q_ref, k_ref, v_ref, g_ref, sb_ref, state_ref, *, d_pool, d_ret):
    @pl.when(pl.program_id(1) == 0)
    def _():
        state_ref[...] = jnp.zeros_like(state_ref)

    tile = x_ref.shape[1]
    x = x_ref[0]
    ms = jnp.mean(x * x, axis=-1, keepdims=True)
    normed = x * lax.rsqrt(ms + NORM_EPS) * gpre_ref[...]
    h = normed * (1.0 + mod_ref[0, 1:2, :]) + mod_ref[0, 0:1, :]
    hb = h.astype(BF16)

    def proj(lo, width):
        return jnp.dot(hb, w_ref[:, lo:lo + width], preferred_element_type=F32)

    def store_blocks(ref, lo, value):
        for i in range(PROJ_PIECE // LANES):
            ref[0, lo // LANES + i] = value[:, i * LANES:(i + 1) * LANES]

    def gate_piece(lo):
        z = proj(d_pool + 3 * d_ret + lo, PROJ_PIECE)
        half_z = 0.5 * z
        gate = half_z * (1.0 + jnp.tanh(half_z))
        if lo < d_pool:
            gate = gate * pscale_ref[:, lo:lo + PROJ_PIECE]
        store_blocks(g_ref, lo, gate.astype(BF16))

    pieces = range(0, d_ret, PROJ_PIECE)
    k_parts, v_parts, q_parts = [], [], []
    gate_los = iter(range(0, g_ref.shape[1] * LANES, PROJ_PIECE))
    for lo in pieces:
        gate_piece(next(gate_los))
        k_parts.append(proj(d_pool + d_ret + lo, PROJ_PIECE))
    for lo in pieces:
        gate_piece(next(gate_los))
        vp = proj(d_pool + 2 * d_ret + lo, PROJ_PIECE).astype(BF16)
        store_blocks(v_ref, lo, vp)
        v_parts.append(vp)
    for lo in pieces:
        q_parts.append(proj(d_pool + lo, PROJ_PIECE))
    for lo in range(0, d_pool, PROJ_PIECE):
        store_blocks(u_ref, lo, proj(lo, PROJ_PIECE).astype(BF16))

    cs = cs_ref[...]
    sn = sn_ref[...]
    k_scale = float(LANES) ** -0.5
    n_chunks = tile // CHUNK
    heads_per_piece = PROJ_PIECE // LANES
    for hd in range(N_RET_HEADS):
        cols = slice(hd * LANES, (hd + 1) * LANES)
        part, local = hd // heads_per_piece, hd % heads_per_piece
        pcols = slice(local * LANES, (local + 1) * LANES)
        qh = q_parts[part][:, pcols]
        q_ref[0, hd] = (qh * cs + pltpu.roll(qh, LANES // 2, 1) * sn).astype(BF16)
        kh = k_parts[part][:, pcols]
        kh = (kh * cs + pltpu.roll(kh, LANES // 2, 1) * sn) * k_scale
        k_ref[0, hd] = kh.astype(BF16)
        vh = v_parts[part][:, pcols]
        for c in reversed(range(n_chunks)):
            rows = slice(c * CHUNK, (c + 1) * CHUNK)
            st = state_ref[hd]
            sb_ref[0, c, hd] = st.astype(BF16)
            kd = (kh[rows] * kb_ref[:, cols]).astype(BF16)
            upd = lax.dot_general(kd, vh[rows], (((0,), (0,)), ((), ())),
                                  preferred_element_type=F32)
            state_ref[hd] = st * cd_ref[1:2, cols] + upd


def _inproj(x, mod, g_pre, w_in_bf16, cs, sn, kb, cd, pool_scale, d_pool, d_ret, d_mix):
    b, s, d = x.shape
    tile = TOKEN_TILE
    n_tiles = s // tile
    n_chunks = tile // CHUNK
    rev = lambda bi, j: (bi, n_tiles - 1 - j, 0)
    blk_spec = lambda width: pl.BlockSpec((1, width // LANES, tile, LANES),
                                          lambda bi, j: (bi, 0, n_tiles - 1 - j, 0))
    blocks = lambda width: jax.ShapeDtypeStruct((b, width // LANES, s, LANES), BF16)
    out_shape = [blocks(d_pool), blocks(d_ret), blocks(d_ret), blocks(d_ret), blocks(d_mix),
                 jax.ShapeDtypeStruct((b, s // CHUNK, N_RET_HEADS, LANES, LANES), BF16)]
    return pl.pallas_call(
        functools.partial(_inproj_kernel, d_pool=d_pool, d_ret=d_ret),
        grid=(b, n_tiles),
        in_specs=[pl.BlockSpec((1, tile, d), rev),
                  pl.BlockSpec((1, 3, d), lambda bi, j: (bi, 0, 0)),
                  _const_spec((1, d)),
                  _const_spec(w_in_bf16.shape),
                  pl.BlockSpec((tile, LANES), lambda bi, j: (n_tiles - 1 - j, 0)),
                  pl.BlockSpec((tile, LANES), lambda bi, j: (n_tiles - 1 - j, 0)),
                  _const_spec(kb.shape),
                  _const_spec(cd.shape),
                  _const_spec((1, d_pool))],
        out_specs=[blk_spec(d_pool), blk_spec(d_ret), blk_spec(d_ret), blk_spec(d_ret),
                   blk_spec(d_mix),
                   pl.BlockSpec((1, n_chunks, N_RET_HEADS, LANES, LANES),
                                lambda bi, j: (bi, n_tiles - 1 - j, 0, 0, 0))],
        out_shape=out_shape,
        scratch_shapes=[pltpu.VMEM((N_RET_HEADS, LANES, LANES), F32)],
        compiler_params=pltpu.CompilerParams(
            dimension_semantics=("arbitrary", "arbitrary"),
            vmem_limit_bytes=VMEM_LIMIT_BYTES),
        name="inproj",
    )(x, mod, g_pre, w_in_bf16, cs, sn, kb, cd, pool_scale)


def _mixer_kernel(x_ref, mod_ref, gpost_ref, u_ref, uprev_ref, unext_ref, q_ref, k_ref, v_ref,
                  g_ref, sb_ref, band_ref, poolw_ref, dmask_ref, qf_ref, qb_ref,
                  kf_ref, cd_ref, wout_ref, y_ref, state_ref, uext_ref, ybuf_ref, *, seq_len):
    j = pl.program_id(1)
    n_tiles = pl.num_programs(1)
    tile = x_ref.shape[1]
    d_pool = u_ref.shape[1] * LANES
    pool_group = d_pool // N_POOL_GROUPS
    group_blocks = pool_group // LANES

    @pl.when(j == 0)
    def _():
        state_ref[...] = jnp.zeros_like(state_ref)

    uprev = uprev_ref[0]
    unext = unext_ref[0]
    uext_ref[:, 0:POOL_HALO] = jnp.where(j == 0, jnp.zeros_like(uprev), uprev)
    uext_ref[:, POOL_HALO:POOL_HALO + tile] = u_ref[0]
    uext_ref[:, POOL_HALO + tile:] = jnp.where(j == n_tiles - 1, jnp.zeros_like(unext), unext)

    head_cols = [slice(hd * LANES, (hd + 1) * LANES) for hd in range(N_RET_HEADS)]
    pool_cols = [slice(g * pool_group, (g + 1) * pool_group) for g in range(N_POOL_GROUPS)]
    pool_blocks = [range(g * group_blocks, (g + 1) * group_blocks) for g in range(N_POOL_GROUPS)]

    def front(c, hd):
        rows = slice(c * CHUNK, (c + 1) * CHUNK)
        cols = head_cols[hd]
        q = q_ref[0, hd, rows, :]
        k = k_ref[0, hd, rows, :]
        v = v_ref[0, hd, rows, :]
        st = state_ref[hd]
        vals = dict(q=q, v=v, st=st.astype(BF16))
        vals["scores"] = lax.dot_general(q, k, (((1,), (1,)), ((), ())), preferred_element_type=F32)
        kd = (k.astype(F32) * kf_ref[:, cols]).astype(BF16)
        upd = lax.dot_general(kd, v, (((0,), (0,)), ((), ())), preferred_element_type=F32)
        state_ref[hd] = st * cd_ref[0:1, cols] + upd
        if hd < N_POOL_GROUPS:
            win = _cat_blocks(uext_ref, (), pool_blocks[hd],
                              slice(c * CHUNK, (c + 1) * CHUNK + 2 * POOL_HALO))
            band = _cat_blocks(band_ref, (hd,), range(band_ref.shape[1]))
            vals["wsum"] = jnp.dot(band, win, preferred_element_type=F32)
        return vals

    def back(c, hd, vals):
        rows = slice(c * CHUNK, (c + 1) * CHUNK)
        cols = head_cols[hd]
        inner = (vals["scores"] * dmask_ref[hd]).astype(BF16)
        o = jnp.dot(inner, vals["v"], preferred_element_type=F32)
        states = jnp.concatenate([vals["st"], sb_ref[0, c, hd]], axis=1)
        cross = jnp.dot(vals["q"], states, preferred_element_type=F32)
        o = o + qf_ref[:, cols] * cross[:, :LANES] + qb_ref[:, cols] * cross[:, LANES:]
        mu = jnp.mean(o, axis=-1, keepdims=True)
        dev = o - mu
        var = jnp.mean(dev * dev, axis=-1, keepdims=True)
        on = dev * lax.rsqrt(var + NORM_EPS)
        zcols = slice(d_pool + hd * LANES, d_pool + (hd + 1) * LANES)
        gate = g_ref[0, d_pool // LANES + hd, rows, :].astype(F32)
        ybuf_ref[rows, zcols] = (on * gate).astype(BF16)
        if hd < N_POOL_GROUPS:
            g, w = hd, POOL_WINDOWS[hd]
            pos = lax.broadcasted_iota(jnp.int32, (CHUNK, 1), 0) + (j * tile + c * CHUNK)
            cnt = jnp.minimum(pos + w // 2, seq_len) - jnp.maximum(pos - w // 2, 0)
            centre = _cat_blocks(uext_ref, (), pool_blocks[g],
                                 slice(POOL_HALO + c * CHUNK, POOL_HALO + (c + 1) * CHUNK)).astype(F32)
            p = vals["wsum"] / cnt.astype(F32) - centre
            pool_w = _cat_blocks(poolw_ref, (g,), range(poolw_ref.shape[1]))
            yp = jnp.dot(p.astype(BF16), pool_w, preferred_element_type=F32)
            pgate = _cat_blocks(g_ref, (0,), pool_blocks[g], rows).astype(F32)
            ybuf_ref[rows, pool_cols[g]] = (yp * pgate).astype(BF16)

    def output_rows(lo, hi):
        w_out = _cat_blocks(wout_ref, (), range(wout_ref.shape[0]))
        out = jnp.dot(ybuf_ref[lo:hi, :], w_out, preferred_element_type=F32)
        ms = jnp.mean(out * out, axis=-1, keepdims=True)
        post = out * lax.rsqrt(ms + NORM_EPS) * gpost_ref[...]
        y_ref[0, lo:hi, :] = x_ref[0, lo:hi, :] + mod_ref[0, 2:3, :] * post

    items = [(c, hd) for c in range(tile // CHUNK) for hd in range(N_RET_HEADS)]
    pending = {}
    for step in range(len(items) + ITEM_LAG):
        if step < len(items):
            pending[step] = front(*items[step])
        done = step - ITEM_LAG
        if done >= 0:
            c, hd = items[done]
            back(c, hd, pending.pop(done))
            if hd == N_RET_HEADS - 1 and c + 1 in OUT_SPLITS:
                prev = ([0] + [s for s in OUT_SPLITS if s < c + 1])[-1]
                output_rows(prev * CHUNK, (c + 1) * CHUNK)


def _band_matrices():
    i = np.arange(CHUNK)[:, None] + POOL_HALO
    m = np.arange(CHUNK + 2 * POOL_HALO)[None, :]
    mats = [((m >= i - w // 2) & (m < i + w // 2)) for w in POOL_WINDOWS]
    return jnp.asarray(np.stack(mats).astype(np.float32), dtype=BF16)


def _mixer(x, mod, g_post, u, q, k, v, g, sb, band, pool_w_blocks, dmask, qf, qb, kf, cd,
           w_out_blocks):
    b, s, d = x.shape
    tile = TOKEN_TILE
    n_tiles = s // tile
    n_chunks = tile // CHUNK
    pool_blks = u.shape[1]
    halo_per_tile = tile // POOL_HALO
    n_halo_blocks = s // POOL_HALO
    fwd = lambda bi, j: (bi, j, 0)
    blk_spec = lambda arr: pl.BlockSpec((1, arr.shape[1], tile, LANES), lambda bi, j: (bi, 0, j, 0))
    prev_map = lambda bi, j: (bi, 0, jnp.maximum(j * halo_per_tile - 1, 0), 0)
    next_map = lambda bi, j: (bi, 0, jnp.minimum((j + 1) * halo_per_tile, n_halo_blocks - 1), 0)
    return pl.pallas_call(
        functools.partial(_mixer_kernel, seq_len=s),
        grid=(b, n_tiles),
        in_specs=[pl.BlockSpec((1, tile, d), fwd),
                  pl.BlockSpec((1, 3, d), lambda bi, j: (bi, 0, 0)),
                  _const_spec((1, d)),
                  blk_spec(u),
                  pl.BlockSpec((1, pool_blks, POOL_HALO, LANES), prev_map),
                  pl.BlockSpec((1, pool_blks, POOL_HALO, LANES), next_map),
                  blk_spec(q), blk_spec(k), blk_spec(v), blk_spec(g),
                  pl.BlockSpec((1, n_chunks, N_RET_HEADS, LANES, LANES),
                               lambda bi, j: (bi, j, 0, 0, 0)),
                  _const_spec(band.shape),
                  _const_spec(pool_w_blocks.shape),
                  _const_spec(dmask.shape),
                  _const_spec(qf.shape), _const_spec(qb.shape), _const_spec(kf.shape),
                  _const_spec(cd.shape),
                  _const_spec(w_out_blocks.shape)],
        out_specs=pl.BlockSpec((1, tile, d), fwd),
        out_shape=jax.ShapeDtypeStruct((b, s, d), F32),
        scratch_shapes=[pltpu.VMEM((N_RET_HEADS, LANES, LANES), F32),
                        pltpu.VMEM((pool_blks, tile + 2 * POOL_HALO, LANES), BF16),
                        pltpu.VMEM((tile, g.shape[1] * LANES), BF16)],
        compiler_params=pltpu.CompilerParams(
            dimension_semantics=("arbitrary", "arbitrary"),
            vmem_limit_bytes=VMEM_LIMIT_BYTES),
        name="mixer",
    )(x, mod, g_post, u, u, u, q, k, v, g, sb, band, pool_w_blocks, dmask, qf, qb, kf, cd,
      w_out_blocks)


def _layer(xs, cs_list, ada_w, ada_b, g_pre, g_post, w_in, pool_w, pool_scale, dec_f, dec_b, w_out,
           rope_cs, rope_sn):
    d = xs[0].shape[2]
    d_pool = pool_scale.shape[0]
    d_mix = w_out.shape[0]
    d_ret = d_mix - d_pool
    assert d_ret == N_RET_HEADS * LANES and d_pool % (N_POOL_GROUPS * LANES) == 0
    assert w_in.shape[1] == d_pool + 3 * d_ret + d_mix

    dmask, qf, qb, kf, kb, cd = _decay_tables(dec_f, dec_b)
    mod = _adaln(jnp.concatenate(cs_list, axis=0), ada_w, ada_b)
    w_in_b = w_in.astype(BF16)
    w_out_blocks = _col_blocks(w_out.astype(BF16))
    pool_w_blocks = _col_blocks(pool_w.astype(BF16))
    band = _col_blocks(_band_matrices())
    g_pre2 = g_pre.reshape(1, d)
    g_post2 = g_post.reshape(1, d)
    pscale2 = pool_scale.reshape(1, d_pool)

    outs = []
    row = 0
    for x in xs:
        b, s, _ = x.shape
        assert s % TOKEN_TILE == 0 and OUT_SPLITS[-1] * CHUNK == TOKEN_TILE
        m = mod[row:row + b].reshape(b, 3, d)
        row += b
        u, q, k, v, g, sb = _inproj(x, m, g_pre2, w_in_b, rope_cs, rope_sn, kb, cd, pscale2,
                                    d_pool, d_ret, d_mix)
        outs.append(_mixer(x, m, g_post2, u, q, k, v, g, sb, band, pool_w_blocks, dmask,
                           qf, qb, kf, cd, w_out_blocks))
    return outs


def kernel(x_prompt, x_sample, c_prompt, c_sample, ada_w, ada_b, norm_pre, norm_post, w_in, pool_w,
           pool_scale, ret_decay_fwd, ret_decay_bwd, w_out):
    xs = [x_prompt, x_sample]
    rope_cs, rope_sn = _rope_tables(max(x.shape[1] for x in xs))
    for l in range(ada_w.shape[0]):
        xs = _layer(xs, [c_prompt, c_sample], ada_w[l], ada_b[l], norm_pre[l], norm_post[l],
                    w_in[l], pool_w[l], pool_scale[l], ret_decay_fwd[l], ret_decay_bwd[l],
                    w_out[l], rope_cs, rope_sn)
    return tuple(xs)
```

```python
import functools

import numpy as np
import jax
import jax.numpy as jnp
from jax import lax
from jax.experimental import pallas as pl
from jax.experimental.pallas import tpu as pltpu

F32 = jnp.float32
BF16 = jnp.bfloat16

POOL_WINDOWS = (2, 4, 8, 16)
N_POOL_GROUPS = len(POOL_WINDOWS)
N_RET_HEADS = 8
CHUNK = 128
ROPE_BASE = 10000.0
NORM_EPS = 1e-6
LANES = 128
POOL_HALO = 64
TOKEN_TILE = 512
PROJ_PIECE = 512
OUT_SPLITS = (2, 4)
ITEM_LAG = 2
VMEM_LIMIT_BYTES = 56 * 1024 * 1024


def _const_spec(shape):
    zeros = (0,) * len(shape)
    return pl.BlockSpec(shape, lambda *_: zeros, pipeline_mode=pl.Buffered(1))


def _sigmoid(x):
    return 1.0 / (1.0 + jnp.exp(-x))


def _col_blocks(x):
    *lead, r, c = x.shape
    return jnp.swapaxes(x.reshape(*lead, r, c // LANES, LANES), -2, -3)


def _split_acts(acts_ref, d_pool, d_ret):
    edges = np.cumsum([0, d_pool, d_ret, d_ret, d_ret]) // LANES
    views = [acts_ref.at[:, int(lo):int(hi)] for lo, hi in zip(edges[:-1], edges[1:])]
    return views + [acts_ref.at[:, int(edges[-1]):]]


def _cat_blocks(ref, idx, blocks, rows=slice(None)):
    return jnp.concatenate([ref[idx + (blk, rows, slice(None))] for blk in blocks], axis=1)


def _rope_table_kernel(freq_ref, cs_ref, sn_ref):
    rows = cs_ref.shape[0]
    base = pl.program_id(0) * rows
    pos = (lax.broadcasted_iota(jnp.int32, (rows, LANES), 0) + base).astype(F32)
    lane = lax.broadcasted_iota(jnp.int32, (rows, LANES), 1)
    ang = pos * freq_ref[...]
    cs_ref[...] = jnp.cos(ang)
    s = jnp.sin(ang)
    sn_ref[...] = jnp.where(lane < LANES // 2, -s, s)


def _rope_tables(seq_len):
    half = LANES // 2
    freqs = ROPE_BASE ** (-jnp.arange(half, dtype=F32) / half)
    freq_row = jnp.concatenate([freqs, freqs]).reshape(1, LANES)
    rows = 512
    return pl.pallas_call(
        _rope_table_kernel,
        grid=(seq_len // rows,),
        in_specs=[pl.BlockSpec((1, LANES), lambda i: (0, 0))],
        out_specs=[pl.BlockSpec((rows, LANES), lambda i: (i, 0))] * 2,
        out_shape=[jax.ShapeDtypeStruct((seq_len, LANES), F32)] * 2,
        name="rope_tables",
    )(freq_row)


def _decay_table_kernel(af_ref, ab_ref, dmask_ref, qf_ref, qb_ref, kf_ref, kb_ref, cd_ref):
    lgf = jnp.log1p(-jnp.exp2(-af_ref[...]))
    lgb = jnp.log1p(-jnp.exp2(-ab_ref[...]))
    width = lgf.shape[1]
    idx = lax.broadcasted_iota(jnp.int32, (CHUNK, width), 0).astype(F32)
    qf_ref[...] = jnp.exp((idx + 1.0) * lgf)
    kf_ref[...] = jnp.exp((CHUNK - 1.0 - idx) * lgf)
    qb_ref[...] = jnp.exp((CHUNK - idx) * lgb)
    kb_ref[...] = jnp.exp(idx * lgb)
    row = lax.broadcasted_iota(jnp.int32, (8, width), 0)
    cd_ref[...] = jnp.where(row == 0, jnp.exp(CHUNK * lgf),
                            jnp.where(row == 1, jnp.exp(CHUNK * lgb), 1.0))
    ii = lax.broadcasted_iota(jnp.int32, (CHUNK, CHUNK), 0)
    jj = lax.broadcasted_iota(jnp.int32, (CHUNK, CHUNK), 1)
    diff = (ii - jj).astype(F32)
    for h in range(N_RET_HEADS):
        cols = slice(h * LANES, (h + 1) * LANES)
        fwd = jnp.exp(jnp.maximum(diff, 0.0) * lgf[:, cols])
        bwd = jnp.exp(jnp.maximum(-diff, 0.0) * lgb[:, cols])
        dmask_ref[h] = jnp.where(diff >= 0, fwd, bwd)


def _decay_tables(dec_f, dec_b):
    width = N_RET_HEADS * LANES
    af = jnp.repeat(dec_f.astype(F32), LANES).reshape(1, width)
    ab = jnp.repeat(dec_b.astype(F32), LANES).reshape(1, width)
    tab = jax.ShapeDtypeStruct((CHUNK, width), F32)
    return pl.pallas_call(
        _decay_table_kernel,
        out_shape=[jax.ShapeDtypeStruct((N_RET_HEADS, CHUNK, CHUNK), F32), tab, tab, tab, tab,
                   jax.ShapeDtypeStruct((8, width), F32)],
        name="decay_tables",
    )(af, ab)


def _adaln_kernel(c_ref, w_ref, b_ref, o_ref):
    c = c_ref[...]
    act = c * _sigmoid(c)
    o_ref[...] = jnp.dot(act, w_ref[...], preferred_element_type=F32,
                         precision=lax.Precision.HIGHEST) + b_ref[...]


def _adaln(c, ada_w, ada_b):
    n, d = c.shape
    e = ada_w.shape[1]
    blk = 1024
    return pl.pallas_call(
        _adaln_kernel,
        grid=(e // blk,),
        in_specs=[pl.BlockSpec((n, d), lambda i: (0, 0)),
                  pl.BlockSpec((d, blk), lambda i: (0, i)),
                  pl.BlockSpec((1, blk), lambda i: (0, i))],
        out_specs=pl.BlockSpec((n, blk), lambda i: (0, i)),
        out_shape=jax.ShapeDtypeStruct((n, e), F32),
        name="adaln",
    )(c, ada_w, ada_b.reshape(1, e))


def _inproj_kernel(x_ref, mod_ref, gpre_ref, w_ref, cs_ref, sn_ref, kb_ref, cd_ref, pscale_ref,
                   acts_ref, sb_ref, state_ref, *, d_pool, d_ret):
    u_ref, q_ref, k_ref, v_ref, g_ref = _split_acts(acts_ref, d_pool, d_ret)

    @pl.when(pl.program_id(1) == 0)
    def _():
        state_ref[...] = jnp.zeros_like(state_ref)

    tile = x_ref.shape[1]
    x = x_ref[0]
    ms = jnp.mean(x * x, axis=-1, keepdims=True)
    normed = x * lax.rsqrt(ms + NORM_EPS) * gpre_ref[...]
    h = normed * (1.0 + mod_ref[0, 1:2, :]) + mod_ref[0, 0:1, :]
    hb = h.astype(BF16)

    def proj(lo, width):
        w = _cat_blocks(w_ref, (), range(lo // LANES, (lo + width) // LANES))
        return jnp.dot(hb, w, preferred_element_type=F32)

    def store_blocks(ref, lo, value):
        for i in range(PROJ_PIECE // LANES):
            ref[0, lo // LANES + i] = value[:, i * LANES:(i + 1) * LANES]

    def gate_piece(lo):
        z = proj(d_pool + 3 * d_ret + lo, PROJ_PIECE)
        half_z = 0.5 * z
        gate = half_z * (1.0 + jnp.tanh(half_z))
        if lo < d_pool:
            gate = gate * pscale_ref[:, lo:lo + PROJ_PIECE]
        store_blocks(g_ref, lo, gate.astype(BF16))

    pieces = range(0, d_ret, PROJ_PIECE)
    k_parts, v_parts = [], []
    gate_los = iter(range(0, g_ref.shape[1] * LANES, PROJ_PIECE))
    for lo in pieces:
        gate_piece(next(gate_los))
        k_parts.append(proj(d_pool + d_ret + lo, PROJ_PIECE))
    for lo in pieces:
        gate_piece(next(gate_los))
        vp = proj(d_pool + 2 * d_ret + lo, PROJ_PIECE).astype(BF16)
        store_blocks(v_ref, lo, vp)
        v_parts.append(vp)

    cs = cs_ref[...]
    sn = sn_ref[...]

    def rotary(t):
        return t * cs + pltpu.roll(t, LANES // 2, 1) * sn

    k_scale = float(LANES) ** -0.5
    n_chunks = tile // CHUNK
    heads_per_piece = PROJ_PIECE // LANES
    for hd in range(N_RET_HEADS):
        cols = slice(hd * LANES, (hd + 1) * LANES)
        part, local = hd // heads_per_piece, hd % heads_per_piece
        pcols = slice(local * LANES, (local + 1) * LANES)
        kh = rotary(k_parts[part][:, pcols]) * k_scale
        k_ref[0, hd] = kh.astype(BF16)
        vh = v_parts[part][:, pcols]
        for c in reversed(range(n_chunks)):
            rows = slice(c * CHUNK, (c + 1) * CHUNK)
            st = state_ref[hd]
            sb_ref[0, c, hd] = st.astype(BF16)
            kd = (kh[rows] * kb_ref[:, cols]).astype(BF16)
            upd = lax.dot_general(kd, vh[rows], (((0,), (0,)), ((), ())),
                                  preferred_element_type=F32)
            state_ref[hd] = st * cd_ref[1:2, cols] + upd

    for lo in pieces:
        qp = proj(d_pool + lo, PROJ_PIECE)
        for i in range(heads_per_piece):
            q_ref[0, lo // LANES + i] = rotary(qp[:, i * LANES:(i + 1) * LANES]).astype(BF16)
    for lo in range(0, d_pool, PROJ_PIECE):
        store_blocks(u_ref, lo, proj(lo, PROJ_PIECE).astype(BF16))


def _inproj(x, mod, g_pre, w_in_bf16, cs, sn, kb, cd, pool_scale, d_pool, d_ret, d_mix):
    b, s, d = x.shape
    tile = TOKEN_TILE
    n_tiles = s // tile
    n_chunks = tile // CHUNK
    rev = lambda bi, j: (bi, n_tiles - 1 - j, 0)
    blk_spec = lambda width: pl.BlockSpec((1, width // LANES, tile, LANES),
                                          lambda bi, j: (bi, 0, n_tiles - 1 - j, 0))
    act_width = d_pool + 3 * d_ret + d_mix
    out_shape = [jax.ShapeDtypeStruct((b, act_width // LANES, s, LANES), BF16),
                 jax.ShapeDtypeStruct((b, s // CHUNK, N_RET_HEADS, LANES, LANES), BF16)]
    return pl.pallas_call(
        functools.partial(_inproj_kernel, d_pool=d_pool, d_ret=d_ret),
        grid=(b, n_tiles),
        in_specs=[pl.BlockSpec((1, tile, d), rev),
                  pl.BlockSpec((1, 3, d), lambda bi, j: (bi, 0, 0)),
                  _const_spec((1, d)),
                  _const_spec(w_in_bf16.shape),
                  pl.BlockSpec((tile, LANES), lambda bi, j: (n_tiles - 1 - j, 0)),
                  pl.BlockSpec((tile, LANES), lambda bi, j: (n_tiles - 1 - j, 0)),
                  _const_spec(kb.shape),
                  _const_spec(cd.shape),
                  _const_spec((1, d_pool))],
        out_specs=[blk_spec(act_width),
                   pl.BlockSpec((1, n_chunks, N_RET_HEADS, LANES, LANES),
                                lambda bi, j: (bi, n_tiles - 1 - j, 0, 0, 0))],
        out_shape=out_shape,
        scratch_shapes=[pltpu.VMEM((N_RET_HEADS, LANES, LANES), F32)],
        compiler_params=pltpu.CompilerParams(
            dimension_semantics=("arbitrary", "arbitrary"),
            vmem_limit_bytes=VMEM_LIMIT_BYTES),
        name="inproj",
    )(x, mod, g_pre, w_in_bf16, cs, sn, kb, cd, pool_scale)


def _mixer_kernel(x_ref, mod_ref, gpost_ref, acts_ref, uprev_ref, unext_ref, sb_ref, band_ref,
                  poolw_ref, dmask_ref, qf_ref, qb_ref, kf_ref, cd_ref, wout_ref, y_ref,
                  state_ref, ybuf_ref, *, seq_len, d_pool, d_ret):
    u_ref, q_ref, k_ref, v_ref, g_ref = _split_acts(acts_ref, d_pool, d_ret)
    j = pl.program_id(1)
    n_tiles = pl.num_programs(1)
    tile = x_ref.shape[1]
    pool_group = d_pool // N_POOL_GROUPS
    group_blocks = pool_group // LANES

    @pl.when(j == 0)
    def _():
        state_ref[...] = jnp.zeros_like(state_ref)

    def pool_window(c, blk):
        lo, hi = c * CHUNK - POOL_HALO, (c + 1) * CHUNK + POOL_HALO
        parts = []
        if lo < 0:
            prev = uprev_ref[0, blk]
            parts.append(jnp.where(j == 0, jnp.zeros_like(prev), prev))
        parts.append(u_ref[0, blk, max(lo, 0):min(hi, tile), :])
        if hi > tile:
            nxt = unext_ref[0, blk]
            parts.append(jnp.where(j == n_tiles - 1, jnp.zeros_like(nxt), nxt))
        return jnp.concatenate(parts, axis=0) if len(parts) > 1 else parts[0]

    head_cols = [slice(hd * LANES, (hd + 1) * LANES) for hd in range(N_RET_HEADS)]
    pool_cols = [slice(g * pool_group, (g + 1) * pool_group) for g in range(N_POOL_GROUPS)]
    pool_blocks = [range(g * group_blocks, (g + 1) * group_blocks) for g in range(N_POOL_GROUPS)]

    def front(c, hd):
        rows = slice(c * CHUNK, (c + 1) * CHUNK)
        cols = head_cols[hd]
        q = q_ref[0, hd, rows, :]
        k = k_ref[0, hd, rows, :]
        v = v_ref[0, hd, rows, :]
        st = state_ref[hd]
        vals = dict(q=q, v=v, st=st)
        vals["scores"] = lax.dot_general(q, k, (((1,), (1,)), ((), ())), preferred_element_type=F32)
        vals["kd_t"] = (k.astype(F32) * kf_ref[:, cols]).astype(BF16).T
        if hd < N_POOL_GROUPS:
            win = jnp.concatenate([pool_window(c, blk) for blk in pool_blocks[hd]], axis=1)
            band = _cat_blocks(band_ref, (hd,), range(band_ref.shape[1]))
            vals["wsum"] = jnp.dot(band, win, preferred_element_type=F32)
        return vals

    def back(c, hd, vals):
        rows = slice(c * CHUNK, (c + 1) * CHUNK)
        cols = head_cols[hd]
        inner = (vals["scores"] * dmask_ref[hd]).astype(BF16)
        stacked = jnp.dot(jnp.concatenate([inner, vals["kd_t"]], axis=0), vals["v"],
                          preferred_element_type=F32)
        o = stacked[:CHUNK]
        st = vals["st"]
        state_ref[hd] = st * cd_ref[0:1, cols] + stacked[CHUNK:]
        states = jnp.concatenate([st.astype(BF16), sb_ref[0, c, hd]], axis=1)
        cross = jnp.dot(vals["q"], states, preferred_element_type=F32)
        o = o + qf_ref[:, cols] * cross[:, :LANES] + qb_ref[:, cols] * cross[:, LANES:]
        mu = jnp.mean(o, axis=-1, keepdims=True)
        dev = o - mu
        var = jnp.mean(dev * dev, axis=-1, keepdims=True)
        on = dev * lax.rsqrt(var + NORM_EPS)
        zcols = slice(d_pool + hd * LANES, d_pool + (hd + 1) * LANES)
        gate = g_ref[0, d_pool // LANES + hd, rows, :].astype(F32)
        ybuf_ref[rows, zcols] = (on * gate).astype(BF16)
        if hd < N_POOL_GROUPS:
            g, w = hd, POOL_WINDOWS[hd]
            pos = lax.broadcasted_iota(jnp.int32, (CHUNK, 1), 0) + (j * tile + c * CHUNK)
            cnt = jnp.minimum(pos + w // 2, seq_len) - jnp.maximum(pos - w // 2, 0)
            centre = _cat_blocks(u_ref, (0,), pool_blocks[g], rows).astype(F32)
            p = vals["wsum"] / cnt.astype(F32) - centre
            pool_w = _cat_blocks(poolw_ref, (g,), range(poolw_ref.shape[1]))
            yp = jnp.dot(p.astype(BF16), pool_w, preferred_element_type=F32)
            pgate = _cat_blocks(g_ref, (0,), pool_blocks[g], rows).astype(F32)
            ybuf_ref[rows, pool_cols[g]] = (yp * pgate).astype(BF16)

    def output_rows(lo, hi):
        w_out = _cat_blocks(wout_ref, (), range(wout_ref.shape[0]))
        out = jnp.dot(ybuf_ref[lo:hi, :], w_out, preferred_element_type=F32)
        ms = jnp.mean(out * out, axis=-1, keepdims=True)
        post = out * lax.rsqrt(ms + NORM_EPS) * gpost_ref[...]
        y_ref[0, lo:hi, :] = x_ref[0, lo:hi, :] + mod_ref[0, 2:3, :] * post

    items = [(c, hd) for c in range(tile // CHUNK) for hd in range(N_RET_HEADS)]
    pending = {}
    for step in range(len(items) + ITEM_LAG):
        if step < len(items):
            pending[step] = front(*items[step])
        done = step - ITEM_LAG
        if done >= 0:
            c, hd = items[done]
            back(c, hd, pending.pop(done))
            if hd == N_RET_HEADS - 1 and c + 1 in OUT_SPLITS:
                prev = ([0] + [s for s in OUT_SPLITS if s < c + 1])[-1]
                output_rows(prev * CHUNK, (c + 1) * CHUNK)


def _band_matrices():
    i = np.arange(CHUNK)[:, None] + POOL_HALO
    m = np.arange(CHUNK + 2 * POOL_HALO)[None, :]
    mats = [((m >= i - w // 2) & (m < i + w // 2)) for w in POOL_WINDOWS]
    return jnp.asarray(np.stack(mats).astype(np.float32), dtype=BF16)


def _mixer(x, mod, g_post, acts, sb, band, pool_w_blocks, dmask, qf, qb, kf, cd, w_out_blocks,
           d_pool, d_ret):
    b, s, d = x.shape
    tile = TOKEN_TILE
    n_tiles = s // tile
    n_chunks = tile // CHUNK
    pool_blks = d_pool // LANES
    halo_per_tile = tile // POOL_HALO
    n_halo_blocks = s // POOL_HALO
    fwd = lambda bi, j: (bi, j, 0)
    prev_map = lambda bi, j: (bi, 0, jnp.maximum(j * halo_per_tile - 1, 0), 0)
    next_map = lambda bi, j: (bi, 0, jnp.minimum((j + 1) * halo_per_tile, n_halo_blocks - 1), 0)
    return pl.pallas_call(
        functools.partial(_mixer_kernel, seq_len=s, d_pool=d_pool, d_ret=d_ret),
        grid=(b, n_tiles),
        in_specs=[pl.BlockSpec((1, tile, d), fwd),
                  pl.BlockSpec((1, 3, d), lambda bi, j: (bi, 0, 0)),
                  _const_spec((1, d)),
                  pl.BlockSpec((1, acts.shape[1], tile, LANES), lambda bi, j: (bi, 0, j, 0)),
                  pl.BlockSpec((1, pool_blks, POOL_HALO, LANES), prev_map),
                  pl.BlockSpec((1, pool_blks, POOL_HALO, LANES), next_map),
                  pl.BlockSpec((1, n_chunks, N_RET_HEADS, LANES, LANES),
                               lambda bi, j: (bi, j, 0, 0, 0)),
                  _const_spec(band.shape),
                  _const_spec(pool_w_blocks.shape),
                  _const_spec(dmask.shape),
                  _const_spec(qf.shape), _const_spec(qb.shape), _const_spec(kf.shape),
                  _const_spec(cd.shape),
                  _const_spec(w_out_blocks.shape)],
        out_specs=pl.BlockSpec((1, tile, d), fwd),
        out_shape=jax.ShapeDtypeStruct((b, s, d), F32),
        scratch_shapes=[pltpu.VMEM((N_RET_HEADS, LANES, LANES), F32),
                        pltpu.VMEM((tile, w_out_blocks.shape[1]), BF16)],
        compiler_params=pltpu.CompilerParams(
            dimension_semantics=("arbitrary", "arbitrary"),
            vmem_limit_bytes=VMEM_LIMIT_BYTES),
        name="mixer",
    )(x, mod, g_post, acts, acts, acts, sb, band, pool_w_blocks, dmask, qf, qb, kf, cd,
      w_out_blocks)


def _layer(xs, cs_list, ada_w, ada_b, g_pre, g_post, w_in, pool_w, pool_scale, dec_f, dec_b, w_out,
           rope_cs, rope_sn):
    d = xs[0].shape[2]
    d_pool = pool_scale.shape[0]
    d_mix = w_out.shape[0]
    d_ret = d_mix - d_pool
    assert d_ret == N_RET_HEADS * LANES and d_pool % (N_POOL_GROUPS * LANES) == 0
    assert w_in.shape[1] == d_pool + 3 * d_ret + d_mix

    dmask, qf, qb, kf, kb, cd = _decay_tables(dec_f, dec_b)
    mod = _adaln(jnp.concatenate(cs_list, axis=0), ada_w, ada_b)
    w_in_b = _col_blocks(w_in.astype(BF16))
    w_out_blocks = _col_blocks(w_out.astype(BF16))
    pool_w_blocks = _col_blocks(pool_w.astype(BF16))
    band = _col_blocks(_band_matrices())
    g_pre2 = g_pre.reshape(1, d)
    g_post2 = g_post.reshape(1, d)
    pscale2 = pool_scale.reshape(1, d_pool)

    outs = []
    row = 0
    for x in xs:
        b, s, _ = x.shape
        assert s % TOKEN_TILE == 0 and OUT_SPLITS[-1] * CHUNK == TOKEN_TILE
        m = mod[row:row + b].reshape(b, 3, d)
        row += b
        acts, sb = _inproj(x, m, g_pre2, w_in_b, rope_cs, rope_sn, kb, cd, pscale2,
                           d_pool, d_ret, d_mix)
        outs.append(_mixer(x, m, g_post2, acts, sb, band, pool_w_blocks, dmask, qf, qb, kf, cd,
                           w_out_blocks, d_pool, d_ret))
    return outs


def kernel(x_prompt, x_sample, c_prompt, c_sample, ada_w, ada_b, norm_pre, norm_post, w_in, pool_w,
           pool_scale, ret_decay_fwd, ret_decay_bwd, w_out):
    xs = [x_prompt, x_sample]
    rope_cs, rope_sn = _rope_tables(max(x.shape[1] for x in xs))
    for l in range(ada_w.shape[0]):
        xs = _layer(xs, [c_prompt, c_sample], ada_w[l], ada_b[l], norm_pre[l], norm_post[l],
                    w_in[l], pool_w[l], pool_scale[l], ret_decay_fwd[l], ret_decay_bwd[l],
                    w_out[l], rope_cs, rope_sn)
    return tuple(xs)
```

```python
import functools

import numpy as np
import jax
import jax.numpy as jnp
from jax import lax
from jax.experimental import pallas as pl
from jax.experimental.pallas import tpu as pltpu

F32 = jnp.float32
BF16 = jnp.bfloat16

POOL_WINDOWS = (2, 4, 8, 16)
N_POOL_GROUPS = len(POOL_WINDOWS)
N_RET_HEADS = 8
CHUNK = 128
POOL_ROWS = 128
ROPE_BASE = 10000.0
NORM_EPS = 1e-6
LANES = 128
POOL_HALO = 64
TOKEN_TILE = 512
PROJ_PIECE = 512
OUT_SPLITS = (2, 4)
ITEM_LAG = 2
VMEM_LIMIT_BYTES = 56 * 1024 * 1024


def _const_spec(shape):
    zeros = (0,) * len(shape)
    return pl.BlockSpec(shape, lambda *_: zeros, pipeline_mode=pl.Buffered(1))


def _sigmoid(x):
    return 1.0 / (1.0 + jnp.exp(-x))


def _col_blocks(x):
    *lead, r, c = x.shape
    return jnp.swapaxes(x.reshape(*lead, r, c // LANES, LANES), -2, -3)


def _split_acts(acts_ref, d_pool, d_ret):
    edges = np.cumsum([0, d_pool, d_ret, d_ret, d_ret]) // LANES
    views = [acts_ref.at[:, int(lo):int(hi)] for lo, hi in zip(edges[:-1], edges[1:])]
    return views + [acts_ref.at[:, int(edges[-1]):]]


def _cat_blocks(ref, idx, blocks, rows=slice(None)):
    return jnp.concatenate([ref[idx + (blk, rows, slice(None))] for blk in blocks], axis=1)


def _rope_table_kernel(freq_ref, cs_ref, sn_ref):
    rows = cs_ref.shape[0]
    base = pl.program_id(0) * rows
    pos = (lax.broadcasted_iota(jnp.int32, (rows, LANES), 0) + base).astype(F32)
    lane = lax.broadcasted_iota(jnp.int32, (rows, LANES), 1)
    ang = pos * freq_ref[...]
    cs_ref[...] = jnp.cos(ang)
    s = jnp.sin(ang)
    sn_ref[...] = jnp.where(lane < LANES // 2, -s, s)


def _rope_tables(seq_len):
    half = LANES // 2
    freqs = ROPE_BASE ** (-jnp.arange(half, dtype=F32) / half)
    freq_row = jnp.concatenate([freqs, freqs]).reshape(1, LANES)
    rows = 512
    return pl.pallas_call(
        _rope_table_kernel,
        grid=(seq_len // rows,),
        in_specs=[pl.BlockSpec((1, LANES), lambda i: (0, 0))],
        out_specs=[pl.BlockSpec((rows, LANES), lambda i: (i, 0))] * 2,
        out_shape=[jax.ShapeDtypeStruct((seq_len, LANES), F32)] * 2,
        name="rope_tables",
    )(freq_row)


def _decay_table_kernel(af_ref, ab_ref, afr_ref, abr_ref, dmask_ref, qf_ref, qb_ref, kf_ref, kb_ref,
                        cd_ref):
    lgf = jnp.log1p(-jnp.exp2(-af_ref[...]))
    lgb = jnp.log1p(-jnp.exp2(-ab_ref[...]))
    width = lgf.shape[1]
    idx = lax.broadcasted_iota(jnp.int32, (CHUNK, width), 0).astype(F32)
    qf_ref[...] = jnp.exp((idx + 1.0) * lgf)
    qb_ref[...] = jnp.exp((CHUNK - idx) * lgb)
    tok = lax.broadcasted_iota(jnp.int32, kf_ref.shape, 1).astype(F32)
    kf_ref[...] = jnp.exp((CHUNK - 1.0 - tok) * jnp.log1p(-jnp.exp2(-afr_ref[...])))
    kb_ref[...] = jnp.exp(tok * jnp.log1p(-jnp.exp2(-abr_ref[...])))
    row = lax.broadcasted_iota(jnp.int32, (8, width), 0)
    cd_ref[...] = jnp.where(row == 0, jnp.exp(CHUNK * lgf),
                            jnp.where(row == 1, jnp.exp(CHUNK * lgb), 1.0))
    ii = lax.broadcasted_iota(jnp.int32, (CHUNK, CHUNK), 0)
    jj = lax.broadcasted_iota(jnp.int32, (CHUNK, CHUNK), 1)
    diff = (ii - jj).astype(F32)
    for h in range(N_RET_HEADS):
        cols = slice(h * LANES, (h + 1) * LANES)
        head_f = jnp.concatenate([lgf[:, cols]] * (CHUNK // LANES), axis=1)
        head_b = jnp.concatenate([lgb[:, cols]] * (CHUNK // LANES), axis=1)
        fwd = jnp.exp(jnp.maximum(diff, 0.0) * head_f)
        bwd = jnp.exp(jnp.maximum(-diff, 0.0) * head_b)
        dmask_ref[h] = jnp.where(diff >= 0, fwd, bwd)


def _decay_tables(dec_f, dec_b):
    width = N_RET_HEADS * LANES
    af = jnp.repeat(dec_f.astype(F32), LANES).reshape(1, width)
    ab = jnp.repeat(dec_b.astype(F32), LANES).reshape(1, width)
    afr = jnp.broadcast_to(dec_f.astype(F32)[:, None], (N_RET_HEADS, CHUNK))
    abr = jnp.broadcast_to(dec_b.astype(F32)[:, None], (N_RET_HEADS, CHUNK))
    tab = jax.ShapeDtypeStruct((CHUNK, width), F32)
    rows = jax.ShapeDtypeStruct((N_RET_HEADS, CHUNK), F32)
    return pl.pallas_call(
        _decay_table_kernel,
        out_shape=[jax.ShapeDtypeStruct((N_RET_HEADS, CHUNK, CHUNK), F32), tab, tab, rows, rows,
                   jax.ShapeDtypeStruct((8, width), F32)],
        name="decay_tables",
    )(af, ab, afr, abr)


def _adaln_kernel(c_ref, w_ref, b_ref, o_ref):
    c = c_ref[...]
    act = c * _sigmoid(c)
    o_ref[...] = jnp.dot(act, w_ref[...], preferred_element_type=F32,
                         precision=lax.Precision.HIGHEST) + b_ref[...]


def _adaln(c, ada_w, ada_b):
    n, d = c.shape
    e = ada_w.shape[1]
    blk = 1024
    return pl.pallas_call(
        _adaln_kernel,
        grid=(e // blk,),
        in_specs=[pl.BlockSpec((n, d), lambda i: (0, 0)),
                  pl.BlockSpec((d, blk), lambda i: (0, i)),
                  pl.BlockSpec((1, blk), lambda i: (0, i))],
        out_specs=pl.BlockSpec((n, blk), lambda i: (0, i)),
        out_shape=jax.ShapeDtypeStruct((n, e), F32),
        name="adaln",
    )(c, ada_w, ada_b.reshape(1, e))


def _inproj_kernel(x_ref, mod_ref, gpre_ref, w_ref, cs_ref, sn_ref, kb_ref, cd_ref, pscale_ref,
                   acts_ref, sb_ref, state_ref, *, d_pool, d_ret):
    u_ref, q_ref, k_ref, v_ref, g_ref = _split_acts(acts_ref, d_pool, d_ret)

    @pl.when(pl.program_id(1) == 0)
    def _():
        state_ref[...] = jnp.zeros_like(state_ref)

    tile = x_ref.shape[1]
    x = x_ref[0]
    ms = jnp.mean(x * x, axis=-1, keepdims=True)
    gain = gpre_ref[...] * (1.0 + mod_ref[0, 1:2, :])
    hb = (x * lax.rsqrt(ms + NORM_EPS) * gain + mod_ref[0, 0:1, :]).astype(BF16)

    def proj(lo, width, row_parts=1):
        w = _cat_blocks(w_ref, (), range(lo // LANES, (lo + width) // LANES))
        step = tile // row_parts
        outs = [jnp.dot(hb[r:r + step], w, preferred_element_type=F32) for r in range(0, tile, step)]
        return outs[0] if row_parts == 1 else jnp.concatenate(outs, axis=0)

    def store_blocks(ref, lo, value):
        for i in range(PROJ_PIECE // LANES):
            ref[0, lo // LANES + i] = value[:, i * LANES:(i + 1) * LANES]

    def gate_piece(lo):
        z = proj(d_pool + 3 * d_ret + lo, PROJ_PIECE, row_parts=2 if lo == 0 else 1)
        half_z = 0.5 * z
        gate = half_z * (1.0 + jnp.tanh(half_z))
        if lo < d_pool:
            gate = gate * pscale_ref[:, lo:lo + PROJ_PIECE]
        store_blocks(g_ref, lo, gate.astype(BF16))

    pieces = range(0, d_ret, PROJ_PIECE)
    k_parts, v_parts = [], []
    gate_los = iter(range(0, g_ref.shape[1] * LANES, PROJ_PIECE))
    for lo in pieces:
        gate_piece(next(gate_los))
        k_parts.append(proj(d_pool + d_ret + lo, PROJ_PIECE))
    for lo in pieces:
        gate_piece(next(gate_los))
        vp = proj(d_pool + 2 * d_ret + lo, PROJ_PIECE).astype(BF16)
        store_blocks(v_ref, lo, vp)
        v_parts.append(vp)

    cs = cs_ref[...]
    sn = sn_ref[...]

    def rotary(t):
        return t * cs + pltpu.roll(t, LANES // 2, 1) * sn

    k_scale = float(LANES) ** -0.5
    n_chunks = tile // CHUNK
    heads_per_piece = PROJ_PIECE // LANES
    for hd in range(N_RET_HEADS):
        cols = slice(hd * LANES, (hd + 1) * LANES)
        part, local = hd // heads_per_piece, hd % heads_per_piece
        pcols = slice(local * LANES, (local + 1) * LANES)
        kh = rotary(k_parts[part][:, pcols]) * k_scale
        vh = v_parts[part][:, pcols]
        for c in reversed(range(n_chunks)):
            rows = slice(c * CHUNK, (c + 1) * CHUNK)
            st = state_ref[hd]
            sb_ref[0, c, hd] = st.astype(BF16)
            kt = kh[rows].T
            k_ref[0, hd, rows, :] = kt.astype(BF16)
            kd_t = (kt * kb_ref[hd:hd + 1, :]).astype(BF16)
            upd = jnp.dot(kd_t, vh[rows], preferred_element_type=F32)
            state_ref[hd] = st * cd_ref[1:2, cols] + upd

    for lo in pieces:
        qp = proj(d_pool + lo, PROJ_PIECE)
        for i in range(heads_per_piece):
            q_ref[0, lo // LANES + i] = rotary(qp[:, i * LANES:(i + 1) * LANES]).astype(BF16)
    for lo in range(0, d_pool, PROJ_PIECE):
        store_blocks(u_ref, lo, proj(lo, PROJ_PIECE).astype(BF16))


def _inproj(x, mod, g_pre, w_in_bf16, cs, sn, kb, cd, pool_scale, d_pool, d_ret, d_mix):
    b, s, d = x.shape
    tile = TOKEN_TILE
    n_tiles = s // tile
    n_chunks = tile // CHUNK
    rev = lambda bi, j: (bi, n_tiles - 1 - j, 0)
    blk_spec = lambda width: pl.BlockSpec((1, width // LANES, tile, LANES),
                                          lambda bi, j: (bi, 0, n_tiles - 1 - j, 0))
    act_width = d_pool + 3 * d_ret + d_mix
    out_shape = [jax.ShapeDtypeStruct((b, act_width // LANES, s, LANES), BF16),
                 jax.ShapeDtypeStruct((b, s // CHUNK, N_RET_HEADS, LANES, LANES), BF16)]
    return pl.pallas_call(
        functools.partial(_inproj_kernel, d_pool=d_pool, d_ret=d_ret),
        grid=(b, n_tiles),
        in_specs=[pl.BlockSpec((1, tile, d), rev),
                  pl.BlockSpec((1, 3, d), lambda bi, j: (bi, 0, 0)),
                  _const_spec((1, d)),
                  _const_spec(w_in_bf16.shape),
                  pl.BlockSpec((tile, LANES), lambda bi, j: (n_tiles - 1 - j, 0)),
                  pl.BlockSpec((tile, LANES), lambda bi, j: (n_tiles - 1 - j, 0)),
                  _const_spec(kb.shape),
                  _const_spec(cd.shape),
                  _const_spec((1, d_pool))],
        out_specs=[blk_spec(act_width),
                   pl.BlockSpec((1, n_chunks, N_RET_HEADS, LANES, LANES),
                                lambda bi, j: (bi, n_tiles - 1 - j, 0, 0, 0))],
        out_shape=out_shape,
        scratch_shapes=[pltpu.VMEM((N_RET_HEADS, LANES, LANES), F32)],
        compiler_params=pltpu.CompilerParams(
            dimension_semantics=("arbitrary", "arbitrary"),
            vmem_limit_bytes=VMEM_LIMIT_BYTES),
        name="inproj",
    )(x, mod, g_pre, w_in_bf16, cs, sn, kb, cd, pool_scale)


def _mixer_kernel(x_ref, mod_ref, gpost_ref, acts_ref, uprev_ref, unext_ref, sb_ref, band_ref,
                  poolw_ref, dmask_ref, qf_ref, qb_ref, kf_ref, cd_ref, wout_ref, y_ref,
                  state_ref, ybuf_ref, *, seq_len, d_pool, d_ret):
    u_ref, q_ref, k_ref, v_ref, g_ref = _split_acts(acts_ref, d_pool, d_ret)
    j = pl.program_id(1)
    n_tiles = pl.num_programs(1)
    tile = x_ref.shape[1]
    pool_group = d_pool // N_POOL_GROUPS
    group_blocks = pool_group // LANES

    @pl.when(j == 0)
    def _():
        state_ref[...] = jnp.zeros_like(state_ref)

    def pool_window(sub, blk):
        lo, hi = sub * POOL_ROWS - POOL_HALO, (sub + 1) * POOL_ROWS + POOL_HALO
        parts = []
        if lo < 0:
            prev = uprev_ref[0, blk]
            parts.append(jnp.where(j == 0, jnp.zeros_like(prev), prev))
        parts.append(u_ref[0, blk, max(lo, 0):min(hi, tile), :])
        if hi > tile:
            nxt = unext_ref[0, blk]
            parts.append(jnp.where(j == n_tiles - 1, jnp.zeros_like(nxt), nxt))
        return jnp.concatenate(parts, axis=0) if len(parts) > 1 else parts[0]

    head_cols = [slice(hd * LANES, (hd + 1) * LANES) for hd in range(N_RET_HEADS)]
    pool_cols = [slice(g * pool_group, (g + 1) * pool_group) for g in range(N_POOL_GROUPS)]
    pool_blocks = [range(g * group_blocks, (g + 1) * group_blocks) for g in range(N_POOL_GROUPS)]

    items = [(c, hd) for c in range(tile // CHUNK) for hd in range(N_RET_HEADS)]
    units = [(sub, g) for sub in range(tile // POOL_ROWS) for g in range(N_POOL_GROUPS)]
    assert len(items) % len(units) == 0 or len(units) % len(items) == 0
    item_units = {it: [u for n, u in enumerate(units) if n * len(items) // len(units) == i]
                  for i, it in enumerate(items)}

    def front(c, hd):
        rows = slice(c * CHUNK, (c + 1) * CHUNK)
        cols = head_cols[hd]
        q = q_ref[0, hd, rows, :]
        kt = k_ref[0, hd, rows, :]
        v = v_ref[0, hd, rows, :]
        st = state_ref[hd]
        vals = dict(q=q, v=v, st=st)
        vals["scores"] = jnp.dot(q, kt, preferred_element_type=F32)
        vals["kd_t"] = (kt.astype(F32) * kf_ref[hd:hd + 1, :]).astype(BF16)
        vals["wsums"] = []
        for sub, g in item_units[(c, hd)]:
            win = jnp.concatenate([pool_window(sub, blk) for blk in pool_blocks[g]], axis=1)
            band = _cat_blocks(band_ref, (g,), range(band_ref.shape[1]))
            vals["wsums"].append(jnp.dot(band, win, preferred_element_type=F32))
        return vals

    def back(c, hd, vals):
        rows = slice(c * CHUNK, (c + 1) * CHUNK)
        cols = head_cols[hd]
        inner = (vals["scores"] * dmask_ref[hd]).astype(BF16)
        stacked = jnp.dot(jnp.concatenate([inner, vals["kd_t"]], axis=0), vals["v"],
                          preferred_element_type=F32)
        o = stacked[:CHUNK]
        st = vals["st"]
        state_ref[hd] = st * cd_ref[0:1, cols] + stacked[CHUNK:]
        states = jnp.concatenate([st.astype(BF16), sb_ref[0, c, hd]], axis=1)
        cross = jnp.dot(vals["q"], states, preferred_element_type=F32)
        o = o + qf_ref[:, cols] * cross[:, :LANES] + qb_ref[:, cols] * cross[:, LANES:]
        mu = jnp.mean(o, axis=-1, keepdims=True)
        dev = o - mu
        var = jnp.mean(dev * dev, axis=-1, keepdims=True)
        on = dev * lax.rsqrt(var + NORM_EPS)
        zcols = slice(d_pool + hd * LANES, d_pool + (hd + 1) * LANES)
        gate = g_ref[0, d_pool // LANES + hd, rows, :].astype(F32)
        ybuf_ref[rows, zcols] = (on * gate).astype(BF16)
        for (sub, g), wsum in zip(item_units[(c, hd)], vals["wsums"]):
            w = POOL_WINDOWS[g]
            prows = slice(sub * POOL_ROWS, (sub + 1) * POOL_ROWS)
            pos = lax.broadcasted_iota(jnp.int32, (POOL_ROWS, 1), 0) + (j * tile + sub * POOL_ROWS)
            cnt = jnp.minimum(pos + w // 2, seq_len) - jnp.maximum(pos - w // 2, 0)
            centre = _cat_blocks(u_ref, (0,), pool_blocks[g], prows).astype(F32)
            p = wsum / cnt.astype(F32) - centre
            pool_w = _cat_blocks(poolw_ref, (g,), range(poolw_ref.shape[1]))
            yp = jnp.dot(p.astype(BF16), pool_w, preferred_element_type=F32)
            pgate = _cat_blocks(g_ref, (0,), pool_blocks[g], prows).astype(F32)
            ybuf_ref[prows, pool_cols[g]] = (yp * pgate).astype(BF16)

    def output_rows(lo, hi):
        w_out = _cat_blocks(wout_ref, (), range(wout_ref.shape[0]))
        out = jnp.dot(ybuf_ref[lo:hi, :], w_out, preferred_element_type=F32)
        ms = jnp.mean(out * out, axis=-1, keepdims=True)
        gain = gpost_ref[...] * mod_ref[0, 2:3, :]
        y_ref[0, lo:hi, :] = x_ref[0, lo:hi, :] + out * lax.rsqrt(ms + NORM_EPS) * gain

    pending = {}
    for step in range(len(items) + ITEM_LAG):
        if step < len(items):
            pending[step] = front(*items[step])
        done = step - ITEM_LAG
        if done >= 0:
            c, hd = items[done]
            back(c, hd, pending.pop(done))
            if hd == N_RET_HEADS - 1 and c + 1 in OUT_SPLITS:
                prev = ([0] + [s for s in OUT_SPLITS if s < c + 1])[-1]
                output_rows(prev * CHUNK, (c + 1) * CHUNK)


def _band_matrices():
    i = np.arange(POOL_ROWS)[:, None] + POOL_HALO
    m = np.arange(POOL_ROWS + 2 * POOL_HALO)[None, :]
    mats = [((m >= i - w // 2) & (m < i + w // 2)) for w in POOL_WINDOWS]
    return jnp.asarray(np.stack(mats).astype(np.float32), dtype=BF16)


def _mixer(x, mod, g_post, acts, sb, band, pool_w_blocks, dmask, qf, qb, kf, cd, w_out_blocks,
           d_pool, d_ret):
    b, s, d = x.shape
    tile = TOKEN_TILE
    n_tiles = s // tile
    n_chunks = tile // CHUNK
    pool_blks = d_pool // LANES
    halo_per_tile = tile // POOL_HALO
    n_halo_blocks = s // POOL_HALO
    fwd = lambda bi, j: (bi, j, 0)
    prev_map = lambda bi, j: (bi, 0, jnp.maximum(j * halo_per_tile - 1, 0), 0)
    next_map = lambda bi, j: (bi, 0, jnp.minimum((j + 1) * halo_per_tile, n_halo_blocks - 1), 0)
    return pl.pallas_call(
        functools.partial(_mixer_kernel, seq_len=s, d_pool=d_pool, d_ret=d_ret),
        grid=(b, n_tiles),
        in_specs=[pl.BlockSpec((1, tile, d), fwd),
                  pl.BlockSpec((1, 3, d), lambda bi, j: (bi, 0, 0)),
                  _const_spec((1, d)),
                  pl.BlockSpec((1, acts.shape[1], tile, LANES), lambda bi, j: (bi, 0, j, 0)),
                  pl.BlockSpec((1, pool_blks, POOL_HALO, LANES), prev_map),
                  pl.BlockSpec((1, pool_blks, POOL_HALO, LANES), next_map),
                  pl.BlockSpec((1, n_chunks, N_RET_HEADS, LANES, LANES),
                               lambda bi, j: (bi, j, 0, 0, 0)),
                  _const_spec(band.shape),
                  _const_spec(pool_w_blocks.shape),
                  _const_spec(dmask.shape),
                  _const_spec(qf.shape), _const_spec(qb.shape), _const_spec(kf.shape),
                  _const_spec(cd.shape),
                  _const_spec(w_out_blocks.shape)],
        out_specs=pl.BlockSpec((1, tile, d), fwd),
        out_shape=jax.ShapeDtypeStruct((b, s, d), F32),
        scratch_shapes=[pltpu.VMEM((N_RET_HEADS, LANES, LANES), F32),
                        pltpu.VMEM((tile, w_out_blocks.shape[1]), BF16)],
        compiler_params=pltpu.CompilerParams(
            dimension_semantics=("arbitrary", "arbitrary"),
            vmem_limit_bytes=VMEM_LIMIT_BYTES),
        name="mixer",
    )(x, mod, g_post, acts, acts, acts, sb, band, pool_w_blocks, dmask, qf, qb, kf, cd,
      w_out_blocks)


def _layer(xs, cs_list, ada_w, ada_b, g_pre, g_post, w_in, pool_w, pool_scale, dec_f, dec_b, w_out,
           rope_cs, rope_sn):
    d = xs[0].shape[2]
    d_pool = pool_scale.shape[0]
    d_mix = w_out.shape[0]
    d_ret = d_mix - d_pool
    assert d_ret == N_RET_HEADS * LANES and d_pool % (N_POOL_GROUPS * LANES) == 0 and CHUNK == LANES
    assert w_in.shape[1] == d_pool + 3 * d_ret + d_mix

    dmask, qf, qb, kf, kb, cd = _decay_tables(dec_f, dec_b)
    mod = _adaln(jnp.concatenate(cs_list, axis=0), ada_w, ada_b)
    w_in_b = _col_blocks(w_in.astype(BF16))
    w_out_blocks = _col_blocks(w_out.astype(BF16))
    pool_w_blocks = _col_blocks(pool_w.astype(BF16))
    band = _col_blocks(_band_matrices())
    g_pre2 = g_pre.reshape(1, d)
    g_post2 = g_post.reshape(1, d)
    pscale2 = pool_scale.reshape(1, d_pool)

    outs = []
    row = 0
    for x in xs:
        b, s, _ = x.shape
        assert s % TOKEN_TILE == 0 and OUT_SPLITS[-1] * CHUNK == TOKEN_TILE
        m = mod[row:row + b].reshape(b, 3, d)
        row += b
        acts, sb = _inproj(x, m, g_pre2, w_in_b, rope_cs, rope_sn, kb, cd, pscale2,
                           d_pool, d_ret, d_mix)
        outs.append(_mixer(x, m, g_post2, acts, sb, band, pool_w_blocks, dmask, qf, qb, kf, cd,
                           w_out_blocks, d_pool, d_ret))
    return outs


def kernel(x_prompt, x_sample, c_prompt, c_sample, ada_w, ada_b, norm_pre, norm_post, w_in, pool_w,
           pool_scale, ret_decay_fwd, ret_decay_bwd, w_out):
    xs = [x_prompt, x_sample]
    rope_cs, rope_sn = _rope_tables(max(x.shape[1] for x in xs))
    for l in range(ada_w.shape[0]):
        xs = _layer(xs, [c_prompt, c_sample], ada_w[l], ada_b[l], norm_pre[l], norm_post[l],
                    w_in[l], pool_w[l], pool_scale[l], ret_decay_fwd[l], ret_decay_bwd[l],
                    w_out[l], rope_cs, rope_sn)
    return tuple(xs)
```

```python
import functools

import numpy as np
import jax
import jax.numpy as jnp
from jax import lax
from jax.experimental import pallas as pl
from jax.experimental.pallas import tpu as pltpu

F32 = jnp.float32
BF16 = jnp.bfloat16

POOL_WINDOWS = (2, 4, 8, 16)
N_POOL_GROUPS = len(POOL_WINDOWS)
N_RET_HEADS = 8
CHUNK = 128
POOL_ROWS = 128
ROPE_BASE = 10000.0
NORM_EPS = 1e-6
LANES = 128
POOL_HALO = 64
TOKEN_TILE = 512
PROJ_PIECE = 512
OUT_SPLITS = (2, 4)
ITEM_LAG = 2
VMEM_LIMIT_BYTES = 56 * 1024 * 1024


def _const_spec(shape):
    zeros = (0,) * len(shape)
    return pl.BlockSpec(shape, lambda *_: zeros, pipeline_mode=pl.Buffered(1))


def _sigmoid(x):
    return 1.0 / (1.0 + jnp.exp(-x))


def _col_blocks(x):
    *lead, r, c = x.shape
    return jnp.swapaxes(x.reshape(*lead, r, c // LANES, LANES), -2, -3)


def _split_acts(acts_ref, d_pool, d_ret):
    edges = np.cumsum([0, d_pool, d_ret, d_ret, d_ret, d_pool + d_ret]) // LANES
    views = [acts_ref.at[:, int(lo):int(hi)] for lo, hi in zip(edges[:-1], edges[1:])]
    return views + [acts_ref.at[:, int(edges[-1]):]]


def _cat_blocks(ref, idx, blocks, rows=slice(None)):
    return jnp.concatenate([ref[idx + (blk, rows, slice(None))] for blk in blocks], axis=1)


def _rope_table_kernel(freq_ref, out_ref):
    rows = out_ref.shape[0]
    base = pl.program_id(0) * rows
    pos = (lax.broadcasted_iota(jnp.int32, (rows, LANES), 0) + base).astype(F32)
    lane = lax.broadcasted_iota(jnp.int32, (rows, LANES), 1)
    ang = pos * freq_ref[...]
    out_ref[:, :LANES] = jnp.cos(ang)
    s = jnp.sin(ang)
    out_ref[:, LANES:] = jnp.where(lane < LANES // 2, -s, s)


def _rope_tables(seq_len):
    half = LANES // 2
    freqs = ROPE_BASE ** (-jnp.arange(half, dtype=F32) / half)
    freq_row = jnp.concatenate([freqs, freqs]).reshape(1, LANES)
    rows = 512
    return pl.pallas_call(
        _rope_table_kernel,
        grid=(seq_len // rows,),
        in_specs=[pl.BlockSpec((1, LANES), lambda i: (0, 0))],
        out_specs=pl.BlockSpec((rows, 2 * LANES), lambda i: (i, 0)),
        out_shape=jax.ShapeDtypeStruct((seq_len, 2 * LANES), F32),
        name="rope_tables",
    )(freq_row)


def _decay_table_kernel(af_ref, ab_ref, dmask_ref, qf_ref, qb_ref, kf_ref, kb_ref, cd_ref):
    lgf = jnp.log1p(-jnp.exp2(-af_ref[...]))
    lgb = jnp.log1p(-jnp.exp2(-ab_ref[...]))
    width = lgf.shape[1]
    idx = lax.broadcasted_iota(jnp.int32, (CHUNK, width), 0).astype(F32)
    qf_ref[...] = jnp.exp((idx + 1.0) * lgf)
    kf_ref[...] = jnp.exp((CHUNK - 1.0 - idx) * lgf)
    qb_ref[...] = jnp.exp((CHUNK - idx) * lgb)
    kb_ref[...] = jnp.exp(idx * lgb)
    row = lax.broadcasted_iota(jnp.int32, (8, width), 0)
    cd_ref[...] = jnp.where(row == 0, jnp.exp(CHUNK * lgf),
                            jnp.where(row == 1, jnp.exp(CHUNK * lgb), 1.0))
    ii = lax.broadcasted_iota(jnp.int32, (CHUNK, CHUNK), 0)
    jj = lax.broadcasted_iota(jnp.int32, (CHUNK, CHUNK), 1)
    diff = (ii - jj).astype(F32)
    for h in range(N_RET_HEADS):
        cols = slice(h * LANES, (h + 1) * LANES)
        head_f = jnp.concatenate([lgf[:, cols]] * (CHUNK // LANES), axis=1)
        head_b = jnp.concatenate([lgb[:, cols]] * (CHUNK // LANES), axis=1)
        fwd = jnp.exp(jnp.maximum(diff, 0.0) * head_f)
        bwd = jnp.exp(jnp.maximum(-diff, 0.0) * head_b)
        dmask_ref[h] = jnp.where(diff >= 0, fwd, bwd)


def _decay_tables(dec_f, dec_b):
    width = N_RET_HEADS * LANES
    af = jnp.repeat(dec_f.astype(F32), LANES).reshape(1, width)
    ab = jnp.repeat(dec_b.astype(F32), LANES).reshape(1, width)
    tab = jax.ShapeDtypeStruct((CHUNK, width), F32)
    return pl.pallas_call(
        _decay_table_kernel,
        out_shape=[jax.ShapeDtypeStruct((N_RET_HEADS, CHUNK, CHUNK), F32), tab, tab, tab, tab,
                   jax.ShapeDtypeStruct((8, width), F32)],
        name="decay_tables",
    )(af, ab)


def _adaln_kernel(c_ref, w_ref, b_ref, o_ref):
    c = c_ref[...]
    act = c * _sigmoid(c)
    o_ref[...] = jnp.dot(act, w_ref[...], preferred_element_type=F32,
                         precision=lax.Precision.HIGHEST) + b_ref[...]


def _adaln(c, ada_w, ada_b):
    n, d = c.shape
    e = ada_w.shape[1]
    blk = 1024
    return pl.pallas_call(
        _adaln_kernel,
        grid=(e // blk,),
        in_specs=[pl.BlockSpec((n, d), lambda i: (0, 0)),
                  pl.BlockSpec((d, blk), lambda i: (0, i)),
                  pl.BlockSpec((1, blk), lambda i: (0, i))],
        out_specs=pl.BlockSpec((n, blk), lambda i: (0, i)),
        out_shape=jax.ShapeDtypeStruct((n, e), F32),
        name="adaln",
    )(c, ada_w, ada_b.reshape(1, e))


def _inproj_kernel(x_ref, mod_ref, gpre_ref, w_ref, rope_ref, kb_ref, cd_ref, pscale_ref,
                   acts_ref, state_ref, *, d_pool, d_ret):
    u_ref, q_ref, k_ref, v_ref, g_ref, sb_ref = _split_acts(acts_ref, d_pool, d_ret)

    @pl.when(pl.program_id(1) == 0)
    def _():
        state_ref[...] = jnp.zeros_like(state_ref)

    tile = x_ref.shape[1]
    x = x_ref[0]
    ms = jnp.mean(x * x, axis=-1, keepdims=True)
    gain = gpre_ref[...] * (1.0 + mod_ref[0, 1:2, :])
    hb = (x * lax.rsqrt(ms + NORM_EPS) * gain + mod_ref[0, 0:1, :]).astype(BF16)

    def proj(lo, width, row_parts=1):
        w = _cat_blocks(w_ref, (), range(lo // LANES, (lo + width) // LANES))
        step = tile // row_parts
        outs = [jnp.dot(hb[r:r + step], w, preferred_element_type=F32) for r in range(0, tile, step)]
        return outs[0] if row_parts == 1 else jnp.concatenate(outs, axis=0)

    def store_blocks(ref, lo, value):
        for i in range(PROJ_PIECE // LANES):
            ref[0, lo // LANES + i] = value[:, i * LANES:(i + 1) * LANES]

    def gate_piece(lo):
        z = proj(d_pool + 3 * d_ret + lo, PROJ_PIECE, row_parts=2 if lo == 0 else 1)
        half_z = 0.5 * z
        gate = half_z * (1.0 + jnp.tanh(half_z))
        if lo < d_pool:
            gate = gate * pscale_ref[:, lo:lo + PROJ_PIECE]
        store_blocks(g_ref, lo, gate.astype(BF16))

    pieces = range(0, d_ret, PROJ_PIECE)
    k_parts, v_parts = [], []
    gate_los = iter(range(0, g_ref.shape[1] * LANES, PROJ_PIECE))
    for lo in pieces:
        gate_piece(next(gate_los))
        k_parts.append(proj(d_pool + d_ret + lo, PROJ_PIECE))
    for lo in pieces:
        gate_piece(next(gate_los))
        vp = proj(d_pool + 2 * d_ret + lo, PROJ_PIECE).astype(BF16)
        store_blocks(v_ref, lo, vp)
        v_parts.append(vp)

    cs = rope_ref[:, :LANES]
    sn = rope_ref[:, LANES:]

    def rotary(t):
        return t * cs + pltpu.roll(t, LANES // 2, 1) * sn

    k_scale = float(LANES) ** -0.5
    n_chunks = tile // CHUNK
    heads_per_piece = PROJ_PIECE // LANES
    for hd in range(N_RET_HEADS):
        cols = slice(hd * LANES, (hd + 1) * LANES)
        part, local = hd // heads_per_piece, hd % heads_per_piece
        pcols = slice(local * LANES, (local + 1) * LANES)
        kh = rotary(k_parts[part][:, pcols]) * k_scale
        k_ref[0, hd] = kh.astype(BF16)
        vh = v_parts[part][:, pcols]
        for c in reversed(range(n_chunks)):
            rows = slice(c * CHUNK, (c + 1) * CHUNK)
            st = state_ref[hd]
            sb_ref[0, hd, rows, :] = st.astype(BF16)
            kd = (kh[rows] * kb_ref[:, cols]).astype(BF16)
            upd = lax.dot_general(kd, vh[rows], (((0,), (0,)), ((), ())),
                                  preferred_element_type=F32)
            state_ref[hd] = st * cd_ref[1:2, cols] + upd

    for lo in pieces:
        qp = proj(d_pool + lo, PROJ_PIECE)
        for i in range(heads_per_piece):
            q_ref[0, lo // LANES + i] = rotary(qp[:, i * LANES:(i + 1) * LANES]).astype(BF16)
    for lo in range(0, d_pool, PROJ_PIECE):
        store_blocks(u_ref, lo, proj(lo, PROJ_PIECE).astype(BF16))


def _inproj(x, mod, g_pre, w_in_bf16, rope, kb, cd, pool_scale, d_pool, d_ret, d_mix):
    b, s, d = x.shape
    tile = TOKEN_TILE
    n_tiles = s // tile
    rev = lambda bi, j: (bi, n_tiles - 1 - j, 0)
    act_blocks = (d_pool + 3 * d_ret + d_mix) // LANES + N_RET_HEADS
    return pl.pallas_call(
        functools.partial(_inproj_kernel, d_pool=d_pool, d_ret=d_ret),
        grid=(b, n_tiles),
        in_specs=[pl.BlockSpec((1, tile, d), rev),
                  pl.BlockSpec((1, 3, d), lambda bi, j: (bi, 0, 0)),
                  _const_spec((1, d)),
                  _const_spec(w_in_bf16.shape),
                  pl.BlockSpec((tile, 2 * LANES), lambda bi, j: (n_tiles - 1 - j, 0)),
                  _const_spec(kb.shape),
                  _const_spec(cd.shape),
                  _const_spec((1, d_pool))],
        out_specs=pl.BlockSpec((1, act_blocks, tile, LANES),
                               lambda bi, j: (bi, 0, n_tiles - 1 - j, 0)),
        out_shape=jax.ShapeDtypeStruct((b, act_blocks, s, LANES), BF16),
        scratch_shapes=[pltpu.VMEM((N_RET_HEADS, LANES, LANES), F32)],
        compiler_params=pltpu.CompilerParams(
            dimension_semantics=("arbitrary", "arbitrary"),
            vmem_limit_bytes=VMEM_LIMIT_BYTES),
        name="inproj",
    )(x, mod, g_pre, w_in_bf16, rope, kb, cd, pool_scale)


def _mixer_kernel(x_ref, mod_ref, gpost_ref, acts_ref, unext_ref, band_ref,
                  poolw_ref, dmask_ref, qf_ref, qb_ref, kf_ref, cd_ref, wout_ref, y_ref,
                  state_ref, ybuf_ref, uprev_ref, *, seq_len, d_pool, d_ret):
    u_ref, q_ref, k_ref, v_ref, g_ref, sb_ref = _split_acts(acts_ref, d_pool, d_ret)
    j = pl.program_id(1)
    n_tiles = pl.num_programs(1)
    tile = x_ref.shape[1]
    pool_group = d_pool // N_POOL_GROUPS
    group_blocks = pool_group // LANES

    @pl.when(j == 0)
    def _():
        state_ref[...] = jnp.zeros_like(state_ref)
        uprev_ref[...] = jnp.zeros_like(uprev_ref)

    def pool_window(sub, blk):
        lo, hi = sub * POOL_ROWS - POOL_HALO, (sub + 1) * POOL_ROWS + POOL_HALO
        parts = []
        if lo < 0:
            parts.append(uprev_ref[blk])
        parts.append(u_ref[0, blk, max(lo, 0):min(hi, tile), :])
        if hi > tile:
            nxt = unext_ref[0, blk]
            parts.append(jnp.where(j == n_tiles - 1, jnp.zeros_like(nxt), nxt))
        return jnp.concatenate(parts, axis=0) if len(parts) > 1 else parts[0]

    head_cols = [slice(hd * LANES, (hd + 1) * LANES) for hd in range(N_RET_HEADS)]
    pool_cols = [slice(g * pool_group, (g + 1) * pool_group) for g in range(N_POOL_GROUPS)]
    pool_blocks = [range(g * group_blocks, (g + 1) * group_blocks) for g in range(N_POOL_GROUPS)]

    items = [(c, hd) for c in range(tile // CHUNK) for hd in range(N_RET_HEADS)]
    units = [(sub, g) for sub in range(tile // POOL_ROWS) for g in range(N_POOL_GROUPS)]
    assert len(items) % len(units) == 0 or len(units) % len(items) == 0
    item_units = {it: [u for n, u in enumerate(units) if n * len(items) // len(units) == i]
                  for i, it in enumerate(items)}

    def front(c, hd):
        rows = slice(c * CHUNK, (c + 1) * CHUNK)
        cols = head_cols[hd]
        q = q_ref[0, hd, rows, :]
        k = k_ref[0, hd, rows, :]
        v = v_ref[0, hd, rows, :]
        st = state_ref[hd]
        vals = dict(q=q, v=v, st=st)
        vals["scores"] = lax.dot_general(q, k, (((1,), (1,)), ((), ())), preferred_element_type=F32)
        vals["kd_t"] = (k.astype(F32) * kf_ref[:, cols]).astype(BF16).T
        vals["wsums"] = []
        for sub, g in item_units[(c, hd)]:
            win = jnp.concatenate([pool_window(sub, blk) for blk in pool_blocks[g]], axis=1)
            band = _cat_blocks(band_ref, (g,), range(band_ref.shape[1]))
            vals["wsums"].append(jnp.dot(band, win, preferred_element_type=F32))
        return vals

    def back(c, hd, vals):
        rows = slice(c * CHUNK, (c + 1) * CHUNK)
        cols = head_cols[hd]
        inner = (vals["scores"] * dmask_ref[hd]).astype(BF16)
        stacked = jnp.dot(jnp.concatenate([inner, vals["kd_t"]], axis=0), vals["v"],
                          preferred_element_type=F32)
        o = stacked[:CHUNK]
        st = vals["st"]
        state_ref[hd] = st * cd_ref[0:1, cols] + stacked[CHUNK:]
        states = jnp.concatenate([st.astype(BF16), sb_ref[0, hd, rows, :]], axis=1)
        cross = jnp.dot(vals["q"], states, preferred_element_type=F32)
        o = o + qf_ref[:, cols] * cross[:, :LANES] + qb_ref[:, cols] * cross[:, LANES:]
        mu = jnp.mean(o, axis=-1, keepdims=True)
        dev = o - mu
        var = jnp.mean(dev * dev, axis=-1, keepdims=True)
        on = dev * lax.rsqrt(var + NORM_EPS)
        zcols = slice(d_pool + hd * LANES, d_pool + (hd + 1) * LANES)
        gate = g_ref[0, d_pool // LANES + hd, rows, :].astype(F32)
        ybuf_ref[rows, zcols] = (on * gate).astype(BF16)
        for (sub, g), wsum in zip(item_units[(c, hd)], vals["wsums"]):
            w = POOL_WINDOWS[g]
            prows = slice(sub * POOL_ROWS, (sub + 1) * POOL_ROWS)
            pos = lax.broadcasted_iota(jnp.int32, (POOL_ROWS, 1), 0) + (j * tile + sub * POOL_ROWS)
            cnt = jnp.minimum(pos + w // 2, seq_len) - jnp.maximum(pos - w // 2, 0)
            centre = _cat_blocks(u_ref, (0,), pool_blocks[g], prows).astype(F32)
            p = wsum / cnt.astype(F32) - centre
            pool_w = _cat_blocks(poolw_ref, (g,), range(poolw_ref.shape[1]))
            yp = jnp.dot(p.astype(BF16), pool_w, preferred_element_type=F32)
            pgate = _cat_blocks(g_ref, (0,), pool_blocks[g], prows).astype(F32)
            ybuf_ref[prows, pool_cols[g]] = (yp * pgate).astype(BF16)

    def output_rows(lo, hi):
        w_out = _cat_blocks(wout_ref, (), range(wout_ref.shape[0]))
        out = jnp.dot(ybuf_ref[lo:hi, :], w_out, preferred_element_type=F32)
        ms = jnp.mean(out * out, axis=-1, keepdims=True)
        gain = gpost_ref[...] * mod_ref[0, 2:3, :]
        y_ref[0, lo:hi, :] = x_ref[0, lo:hi, :] + out * lax.rsqrt(ms + NORM_EPS) * gain

    pending = {}
    for step in range(len(items) + ITEM_LAG):
        if step < len(items):
            pending[step] = front(*items[step])
        done = step - ITEM_LAG
        if done >= 0:
            c, hd = items[done]
            back(c, hd, pending.pop(done))
            if hd == N_RET_HEADS - 1 and c + 1 in OUT_SPLITS:
                prev = ([0] + [s for s in OUT_SPLITS if s < c + 1])[-1]
                output_rows(prev * CHUNK, (c + 1) * CHUNK)

    uprev_ref[...] = u_ref[0, :, tile - POOL_HALO:tile, :]


def _band_matrices():
    i = np.arange(POOL_ROWS)[:, None] + POOL_HALO
    m = np.arange(POOL_ROWS + 2 * POOL_HALO)[None, :]
    mats = [((m >= i - w // 2) & (m < i + w // 2)) for w in POOL_WINDOWS]
    return jnp.asarray(np.stack(mats).astype(np.float32), dtype=BF16)


def _mixer(x, mod, g_post, acts, band, pool_w_blocks, dmask, qf, qb, kf, cd, w_out_blocks,
           d_pool, d_ret):
    b, s, d = x.shape
    tile = TOKEN_TILE
    n_tiles = s // tile
    pool_blks = d_pool // LANES
    halo_per_tile = tile // POOL_HALO
    n_halo_blocks = s // POOL_HALO
    fwd = lambda bi, j: (bi, j, 0)
    next_map = lambda bi, j: (bi, 0, jnp.minimum((j + 1) * halo_per_tile, n_halo_blocks - 1), 0)
    return pl.pallas_call(
        functools.partial(_mixer_kernel, seq_len=s, d_pool=d_pool, d_ret=d_ret),
        grid=(b, n_tiles),
        in_specs=[pl.BlockSpec((1, tile, d), fwd),
                  pl.BlockSpec((1, 3, d), lambda bi, j: (bi, 0, 0)),
                  _const_spec((1, d)),
                  pl.BlockSpec((1, acts.shape[1], tile, LANES), lambda bi, j: (bi, 0, j, 0)),
                  pl.BlockSpec((1, pool_blks, POOL_HALO, LANES), next_map),
                  _const_spec(band.shape),
                  _const_spec(pool_w_blocks.shape),
                  _const_spec(dmask.shape),
                  _const_spec(qf.shape), _const_spec(qb.shape), _const_spec(kf.shape),
                  _const_spec(cd.shape),
                  _const_spec(w_out_blocks.shape)],
        out_specs=pl.BlockSpec((1, tile, d), fwd),
        out_shape=jax.ShapeDtypeStruct((b, s, d), F32),
        scratch_shapes=[pltpu.VMEM((N_RET_HEADS, LANES, LANES), F32),
                        pltpu.VMEM((tile, w_out_blocks.shape[1]), BF16),
                        pltpu.VMEM((pool_blks, POOL_HALO, LANES), BF16)],
        compiler_params=pltpu.CompilerParams(
            dimension_semantics=("arbitrary", "arbitrary"),
            vmem_limit_bytes=VMEM_LIMIT_BYTES),
        name="mixer",
    )(x, mod, g_post, acts, acts, band, pool_w_blocks, dmask, qf, qb, kf, cd, w_out_blocks)


def _layer(xs, cs_list, ada_w, ada_b, g_pre, g_post, w_in, pool_w, pool_scale, dec_f, dec_b, w_out,
           rope):
    d = xs[0].shape[2]
    d_pool = pool_scale.shape[0]
    d_mix = w_out.shape[0]
    d_ret = d_mix - d_pool
    assert d_ret == N_RET_HEADS * LANES and d_pool % (N_POOL_GROUPS * LANES) == 0
    assert w_in.shape[1] == d_pool + 3 * d_ret + d_mix

    dmask, qf, qb, kf, kb, cd = _decay_tables(dec_f, dec_b)
    mod = _adaln(jnp.concatenate(cs_list, axis=0), ada_w, ada_b)
    w_in_b = _col_blocks(w_in.astype(BF16))
    w_out_blocks = _col_blocks(w_out.astype(BF16))
    pool_w_blocks = _col_blocks(pool_w.astype(BF16))
    band = _col_blocks(_band_matrices())
    g_pre2 = g_pre.reshape(1, d)
    g_post2 = g_post.reshape(1, d)
    pscale2 = pool_scale.reshape(1, d_pool)

    outs = []
    row = 0
    for x in xs:
        b, s, _ = x.shape
        assert s % TOKEN_TILE == 0 and OUT_SPLITS[-1] * CHUNK == TOKEN_TILE
        m = mod[row:row + b].reshape(b, 3, d)
        row += b
        acts = _inproj(x, m, g_pre2, w_in_b, rope, kb, cd, pscale2, d_pool, d_ret, d_mix)
        outs.append(_mixer(x, m, g_post2, acts, band, pool_w_blocks, dmask, qf, qb, kf, cd,
                           w_out_blocks, d_pool, d_ret))
    return outs


def kernel(x_prompt, x_sample, c_prompt, c_sample, ada_w, ada_b, norm_pre, norm_post, w_in, pool_w,
           pool_scale, ret_decay_fwd, ret_decay_bwd, w_out):
    xs = [x_prompt, x_sample]
    rope = _rope_tables(max(x.shape[1] for x in xs))
    for l in range(ada_w.shape[0]):
        xs = _layer(xs, [c_prompt, c_sample], ada_w[l], ada_b[l], norm_pre[l], norm_post[l],
                    w_in[l], pool_w[l], pool_scale[l], ret_decay_fwd[l], ret_decay_bwd[l],
                    w_out[l], rope)
    return tuple(xs)
```

```python
import functools

import numpy as np
import jax
import jax.numpy as jnp
from jax import lax
from jax.experimental import pallas as pl
from jax.experimental.pallas import tpu as pltpu

F32 = jnp.float32
BF16 = jnp.bfloat16

POOL_WINDOWS = (2, 4, 8, 16)
N_POOL_GROUPS = len(POOL_WINDOWS)
N_RET_HEADS = 8
CHUNK = 128
POOL_ROWS = 128
ROPE_BASE = 10000.0
NORM_EPS = 1e-6
LANES = 128
POOL_HALO = 64
TOKEN_TILE = 512
PROJ_PIECE = 512
OUT_SPLITS = (4,)
ITEM_LAG = 2
VMEM_LIMIT_BYTES = 56 * 1024 * 1024


def _const_spec(shape):
    zeros = (0,) * len(shape)
    return pl.BlockSpec(shape, lambda *_: zeros, pipeline_mode=pl.Buffered(1))


def _sigmoid(x):
    return 1.0 / (1.0 + jnp.exp(-x))


def _col_blocks(x):
    *lead, r, c = x.shape
    return jnp.swapaxes(x.reshape(*lead, r, c // LANES, LANES), -2, -3)


def _split_acts(acts_ref, d_pool, d_ret):
    edges = np.cumsum([0, d_pool, d_ret, d_ret, d_ret, d_pool + d_ret]) // LANES
    views = [acts_ref.at[:, int(lo):int(hi)] for lo, hi in zip(edges[:-1], edges[1:])]
    return views + [acts_ref.at[:, int(edges[-1]):]]


def _cat_blocks(ref, idx, blocks, rows=slice(None)):
    return jnp.concatenate([ref[idx + (blk, rows, slice(None))] for blk in blocks], axis=1)


def _rope_table_kernel(freq_ref, out_ref):
    rows = out_ref.shape[0]
    base = pl.program_id(0) * rows
    pos = (lax.broadcasted_iota(jnp.int32, (rows, LANES), 0) + base).astype(F32)
    lane = lax.broadcasted_iota(jnp.int32, (rows, LANES), 1)
    ang = pos * freq_ref[...]
    out_ref[:, :LANES] = jnp.cos(ang)
    s = jnp.sin(ang)
    out_ref[:, LANES:] = jnp.where(lane < LANES // 2, -s, s)


def _rope_tables(seq_len):
    half = LANES // 2
    freqs = ROPE_BASE ** (-jnp.arange(half, dtype=F32) / half)
    freq_row = jnp.concatenate([freqs, freqs]).reshape(1, LANES)
    rows = 512
    return pl.pallas_call(
        _rope_table_kernel,
        grid=(seq_len // rows,),
        in_specs=[pl.BlockSpec((1, LANES), lambda i: (0, 0))],
        out_specs=pl.BlockSpec((rows, 2 * LANES), lambda i: (i, 0)),
        out_shape=jax.ShapeDtypeStruct((seq_len, 2 * LANES), F32),
        name="rope_tables",
    )(freq_row)


def _decay_table_kernel(af_ref, ab_ref, dmask_ref, qf_ref, qb_ref, kf_ref, kb_ref, cd_ref):
    lgf = jnp.log1p(-jnp.exp2(-af_ref[...]))
    lgb = jnp.log1p(-jnp.exp2(-ab_ref[...]))
    width = lgf.shape[1]
    idx = lax.broadcasted_iota(jnp.int32, (CHUNK, width), 0).astype(F32)
    qf_ref[...] = jnp.exp((idx + 1.0) * lgf)
    kf_ref[...] = jnp.exp((CHUNK - 1.0 - idx) * lgf)
    qb_ref[...] = jnp.exp((CHUNK - idx) * lgb)
    kb_ref[...] = jnp.exp(idx * lgb)
    row = lax.broadcasted_iota(jnp.int32, (8, width), 0)
    cd_ref[...] = jnp.where(row == 0, jnp.exp(CHUNK * lgf),
                            jnp.where(row == 1, jnp.exp(CHUNK * lgb), 1.0))
    ii = lax.broadcasted_iota(jnp.int32, (CHUNK, CHUNK), 0)
    jj = lax.broadcasted_iota(jnp.int32, (CHUNK, CHUNK), 1)
    diff = (ii - jj).astype(F32)
    for h in range(N_RET_HEADS):
        cols = slice(h * LANES, (h + 1) * LANES)
        head_f = jnp.concatenate([lgf[:, cols]] * (CHUNK // LANES), axis=1)
        head_b = jnp.concatenate([lgb[:, cols]] * (CHUNK // LANES), axis=1)
        fwd = jnp.exp(jnp.maximum(diff, 0.0) * head_f)
        bwd = jnp.exp(jnp.maximum(-diff, 0.0) * head_b)
        dmask_ref[h] = jnp.where(diff >= 0, fwd, bwd)


def _decay_tables(dec_f, dec_b):
    width = N_RET_HEADS * LANES
    af = jnp.repeat(dec_f.astype(F32), LANES).reshape(1, width)
    ab = jnp.repeat(dec_b.astype(F32), LANES).reshape(1, width)
    tab = jax.ShapeDtypeStruct((CHUNK, width), F32)
    return pl.pallas_call(
        _decay_table_kernel,
        out_shape=[jax.ShapeDtypeStruct((N_RET_HEADS, CHUNK, CHUNK), F32), tab, tab, tab, tab,
                   jax.ShapeDtypeStruct((8, width), F32)],
        name="decay_tables",
    )(af, ab)


def _adaln_kernel(c_ref, w_ref, b_ref, o_ref):
    c = c_ref[...]
    act = c * _sigmoid(c)
    o_ref[...] = jnp.dot(act, w_ref[...], preferred_element_type=F32,
                         precision=lax.Precision.HIGHEST) + b_ref[...]


def _adaln(c, ada_w, ada_b):
    n, d = c.shape
    e = ada_w.shape[1]
    blk = 1024
    return pl.pallas_call(
        _adaln_kernel,
        grid=(e // blk,),
        in_specs=[pl.BlockSpec((n, d), lambda i: (0, 0)),
                  pl.BlockSpec((d, blk), lambda i: (0, i)),
                  pl.BlockSpec((1, blk), lambda i: (0, i))],
        out_specs=pl.BlockSpec((n, blk), lambda i: (0, i)),
        out_shape=jax.ShapeDtypeStruct((n, e), F32),
        name="adaln",
    )(c, ada_w, ada_b.reshape(1, e))


def _inproj_kernel(x_ref, mod_ref, gpre_ref, w_ref, rope_ref, kb_ref, cd_ref, pscale_ref,
                   acts_ref, state_ref, *, d_pool, d_ret):
    u_ref, q_ref, k_ref, v_ref, g_ref, sb_ref = _split_acts(acts_ref, d_pool, d_ret)

    @pl.when(pl.program_id(1) == 0)
    def _():
        state_ref[...] = jnp.zeros_like(state_ref)

    tile = x_ref.shape[1]
    x = x_ref[0]
    ms = jnp.mean(x * x, axis=-1, keepdims=True)
    gain = gpre_ref[...] * (1.0 + mod_ref[0, 1:2, :])
    hb = (x * lax.rsqrt(ms + NORM_EPS) * gain + mod_ref[0, 0:1, :]).astype(BF16)

    def proj(lo, width, row_parts=1):
        w = _cat_blocks(w_ref, (), range(lo // LANES, (lo + width) // LANES))
        step = tile // row_parts
        outs = [jnp.dot(hb[r:r + step], w, preferred_element_type=F32) for r in range(0, tile, step)]
        return outs[0] if row_parts == 1 else jnp.concatenate(outs, axis=0)

    def store_blocks(ref, lo, value):
        for i in range(PROJ_PIECE // LANES):
            ref[0, lo // LANES + i] = value[:, i * LANES:(i + 1) * LANES]

    def gate_piece(lo):
        z = proj(d_pool + 3 * d_ret + lo, PROJ_PIECE, row_parts=2 if lo == 0 else 1)
        half_z = 0.5 * z
        gate = half_z * (1.0 + jnp.tanh(half_z))
        if lo < d_pool:
            gate = gate * pscale_ref[:, lo:lo + PROJ_PIECE]
        store_blocks(g_ref, lo, gate.astype(BF16))

    pieces = range(0, d_ret, PROJ_PIECE)
    k_parts, v_parts = [], []
    gate_los = iter(range(0, g_ref.shape[1] * LANES, PROJ_PIECE))
    for lo in pieces:
        gate_piece(next(gate_los))
        k_parts.append(proj(d_pool + d_ret + lo, PROJ_PIECE))
    for lo in pieces:
        gate_piece(next(gate_los))
        vp = proj(d_pool + 2 * d_ret + lo, PROJ_PIECE).astype(BF16)
        store_blocks(v_ref, lo, vp)
        v_parts.append(vp)

    cs = rope_ref[:, :LANES]
    sn = rope_ref[:, LANES:]

    def rotary(t):
        return t * cs + pltpu.roll(t, LANES // 2, 1) * sn

    k_scale = float(LANES) ** -0.5
    n_chunks = tile // CHUNK
    heads_per_piece = PROJ_PIECE // LANES
    for hd in range(N_RET_HEADS):
        cols = slice(hd * LANES, (hd + 1) * LANES)
        part, local = hd // heads_per_piece, hd % heads_per_piece
        pcols = slice(local * LANES, (local + 1) * LANES)
        kh = rotary(k_parts[part][:, pcols]) * k_scale
        k_ref[0, hd] = kh.astype(BF16)
        vh = v_parts[part][:, pcols]
        for c in reversed(range(n_chunks)):
            rows = slice(c * CHUNK, (c + 1) * CHUNK)
            st = state_ref[hd]
            sb_ref[0, hd, rows, :] = st.astype(BF16)
            kd = (kh[rows] * kb_ref[:, cols]).astype(BF16)
            upd = lax.dot_general(kd, vh[rows], (((0,), (0,)), ((), ())),
                                  preferred_element_type=F32)
            state_ref[hd] = st * cd_ref[1:2, cols] + upd

    for lo in pieces:
        qp = proj(d_pool + lo, PROJ_PIECE)
        for i in range(heads_per_piece):
            q_ref[0, lo // LANES + i] = rotary(qp[:, i * LANES:(i + 1) * LANES]).astype(BF16)
    for lo in range(0, d_pool, PROJ_PIECE):
        store_blocks(u_ref, lo, proj(lo, PROJ_PIECE).astype(BF16))


def _inproj(x, mod, g_pre, w_in_bf16, rope, kb, cd, pool_scale, d_pool, d_ret, d_mix):
    b, s, d = x.shape
    tile = TOKEN_TILE
    n_tiles = s // tile
    rev = lambda bi, j: (bi, n_tiles - 1 - j, 0)
    act_blocks = (d_pool + 3 * d_ret + d_mix) // LANES + N_RET_HEADS
    return pl.pallas_call(
        functools.partial(_inproj_kernel, d_pool=d_pool, d_ret=d_ret),
        grid=(b, n_tiles),
        in_specs=[pl.BlockSpec((1, tile, d), rev),
                  pl.BlockSpec((1, 3, d), lambda bi, j: (bi, 0, 0)),
                  _const_spec((1, d)),
                  _const_spec(w_in_bf16.shape),
                  pl.BlockSpec((tile, 2 * LANES), lambda bi, j: (n_tiles - 1 - j, 0)),
                  _const_spec(kb.shape),
                  _const_spec(cd.shape),
                  _const_spec((1, d_pool))],
        out_specs=pl.BlockSpec((1, act_blocks, tile, LANES),
                               lambda bi, j: (bi, 0, n_tiles - 1 - j, 0)),
        out_shape=jax.ShapeDtypeStruct((b, act_blocks, s, LANES), BF16),
        scratch_shapes=[pltpu.VMEM((N_RET_HEADS, LANES, LANES), F32)],
        compiler_params=pltpu.CompilerParams(
            dimension_semantics=("arbitrary", "arbitrary"),
            vmem_limit_bytes=VMEM_LIMIT_BYTES),
        name="inproj",
    )(x, mod, g_pre, w_in_bf16, rope, kb, cd, pool_scale)


def _mixer_kernel(x_ref, mod_ref, gpost_ref, acts_ref, unext_ref, band_ref,
                  poolw_ref, dmask_ref, qf_ref, qb_ref, kf_ref, cd_ref, wout_ref, y_ref,
                  state_ref, ybuf_ref, uprev_ref, *, seq_len, d_pool, d_ret):
    u_ref, q_ref, k_ref, v_ref, g_ref, sb_ref = _split_acts(acts_ref, d_pool, d_ret)
    j = pl.program_id(1)
    n_tiles = pl.num_programs(1)
    tile = x_ref.shape[1]
    pool_group = d_pool // N_POOL_GROUPS
    group_blocks = pool_group // LANES

    @pl.when(j == 0)
    def _():
        state_ref[...] = jnp.zeros_like(state_ref)
        uprev_ref[...] = jnp.zeros_like(uprev_ref)

    def pool_window(sub, blk):
        lo, hi = sub * POOL_ROWS - POOL_HALO, (sub + 1) * POOL_ROWS + POOL_HALO
        parts = []
        if lo < 0:
            parts.append(uprev_ref[blk])
        parts.append(u_ref[0, blk, max(lo, 0):min(hi, tile), :])
        if hi > tile:
            nxt = unext_ref[0, blk]
            parts.append(jnp.where(j == n_tiles - 1, jnp.zeros_like(nxt), nxt))
        return jnp.concatenate(parts, axis=0) if len(parts) > 1 else parts[0]

    head_cols = [slice(hd * LANES, (hd + 1) * LANES) for hd in range(N_RET_HEADS)]
    pool_cols = [slice(g * pool_group, (g + 1) * pool_group) for g in range(N_POOL_GROUPS)]
    pool_blocks = [range(g * group_blocks, (g + 1) * group_blocks) for g in range(N_POOL_GROUPS)]

    items = [(c, hd) for c in range(tile // CHUNK) for hd in range(N_RET_HEADS)]
    units = [(sub, g) for sub in range(tile // POOL_ROWS) for g in range(N_POOL_GROUPS)]
    assert len(items) % len(units) == 0 or len(units) % len(items) == 0
    item_units = {it: [u for n, u in enumerate(units) if n * len(items) // len(units) == i]
                  for i, it in enumerate(items)}

    def front(c, hd):
        rows = slice(c * CHUNK, (c + 1) * CHUNK)
        cols = head_cols[hd]
        q = q_ref[0, hd, rows, :]
        k = k_ref[0, hd, rows, :]
        v = v_ref[0, hd, rows, :]
        st = state_ref[hd]
        vals = dict(q=q, v=v, st=st)
        vals["scores"] = lax.dot_general(q, k, (((1,), (1,)), ((), ())), preferred_element_type=F32)
        vals["kd_t"] = (k.astype(F32) * kf_ref[:, cols]).astype(BF16).T
        vals["wsums"] = []
        for sub, g in item_units[(c, hd)]:
            win = jnp.concatenate([pool_window(sub, blk) for blk in pool_blocks[g]], axis=1)
            band = _cat_blocks(band_ref, (g,), range(band_ref.shape[1]))
            vals["wsums"].append(jnp.dot(band, win, preferred_element_type=F32))
        return vals

    def back(c, hd, vals):
        rows = slice(c * CHUNK, (c + 1) * CHUNK)
        cols = head_cols[hd]
        inner = (vals["scores"] * dmask_ref[hd]).astype(BF16)
        stacked = jnp.dot(jnp.concatenate([inner, vals["kd_t"]], axis=0), vals["v"],
                          preferred_element_type=F32)
        o = stacked[:CHUNK]
        st = vals["st"]
        state_ref[hd] = st * cd_ref[0:1, cols] + stacked[CHUNK:]
        states = jnp.concatenate([st.astype(BF16), sb_ref[0, hd, rows, :]], axis=1)
        cross = jnp.dot(vals["q"], states, preferred_element_type=F32)
        o = o + qf_ref[:, cols] * cross[:, :LANES] + qb_ref[:, cols] * cross[:, LANES:]
        mu = jnp.mean(o, axis=-1, keepdims=True)
        dev = o - mu
        var = jnp.mean(dev * dev, axis=-1, keepdims=True)
        on = dev * lax.rsqrt(var + NORM_EPS)
        zcols = slice(d_pool + hd * LANES, d_pool + (hd + 1) * LANES)
        gate = g_ref[0, d_pool // LANES + hd, rows, :].astype(F32)
        ybuf_ref[rows, zcols] = (on * gate).astype(BF16)
        for (sub, g), wsum in zip(item_units[(c, hd)], vals["wsums"]):
            w = POOL_WINDOWS[g]
            prows = slice(sub * POOL_ROWS, (sub + 1) * POOL_ROWS)
            pos = lax.broadcasted_iota(jnp.int32, (POOL_ROWS, 1), 0) + (j * tile + sub * POOL_ROWS)
            cnt = jnp.minimum(pos + w // 2, seq_len) - jnp.maximum(pos - w // 2, 0)
            centre = _cat_blocks(u_ref, (0,), pool_blocks[g], prows).astype(F32)
            p = wsum / cnt.astype(F32) - centre
            pool_w = _cat_blocks(poolw_ref, (g,), range(poolw_ref.shape[1]))
            yp = jnp.dot(p.astype(BF16), pool_w, preferred_element_type=F32)
            pgate = _cat_blocks(g_ref, (0,), pool_blocks[g], prows).astype(F32)
            ybuf_ref[prows, pool_cols[g]] = (yp * pgate).astype(BF16)

    def output_rows(lo, hi):
        w_out = _cat_blocks(wout_ref, (), range(wout_ref.shape[0]))
        out = jnp.dot(ybuf_ref[lo:hi, :], w_out, preferred_element_type=F32)
        ms = jnp.mean(out * out, axis=-1, keepdims=True)
        gain = gpost_ref[...] * mod_ref[0, 2:3, :]
        y_ref[0, lo:hi, :] = x_ref[0, lo:hi, :] + out * lax.rsqrt(ms + NORM_EPS) * gain

    pending = {}
    for step in range(len(items) + ITEM_LAG):
        if step < len(items):
            pending[step] = front(*items[step])
        done = step - ITEM_LAG
        if done >= 0:
            c, hd = items[done]
            back(c, hd, pending.pop(done))
            if hd == N_RET_HEADS - 1 and c + 1 in OUT_SPLITS:
                prev = ([0] + [s for s in OUT_SPLITS if s < c + 1])[-1]
                output_rows(prev * CHUNK, (c + 1) * CHUNK)

    uprev_ref[...] = u_ref[0, :, tile - POOL_HALO:tile, :]


def _band_matrices():
    i = np.arange(POOL_ROWS)[:, None] + POOL_HALO
    m = np.arange(POOL_ROWS + 2 * POOL_HALO)[None, :]
    mats = [((m >= i - w // 2) & (m < i + w // 2)) for w in POOL_WINDOWS]
    return jnp.asarray(np.stack(mats).astype(np.float32), dtype=BF16)


def _mixer(x, mod, g_post, acts, band, pool_w_blocks, dmask, qf, qb, kf, cd, w_out_blocks,
           d_pool, d_ret):
    b, s, d = x.shape
    tile = TOKEN_TILE
    n_tiles = s // tile
    pool_blks = d_pool // LANES
    halo_per_tile = tile // POOL_HALO
    n_halo_blocks = s // POOL_HALO
    fwd = lambda bi, j: (bi, j, 0)
    next_map = lambda bi, j: (bi, 0, jnp.minimum((j + 1) * halo_per_tile, n_halo_blocks - 1), 0)
    return pl.pallas_call(
        functools.partial(_mixer_kernel, seq_len=s, d_pool=d_pool, d_ret=d_ret),
        grid=(b, n_tiles),
        in_specs=[pl.BlockSpec((1, tile, d), fwd),
                  pl.BlockSpec((1, 3, d), lambda bi, j: (bi, 0, 0)),
                  _const_spec((1, d)),
                  pl.BlockSpec((1, acts.shape[1], tile, LANES), lambda bi, j: (bi, 0, j, 0)),
                  pl.BlockSpec((1, pool_blks, POOL_HALO, LANES), next_map),
                  _const_spec(band.shape),
                  _const_spec(pool_w_blocks.shape),
                  _const_spec(dmask.shape),
                  _const_spec(qf.shape), _const_spec(qb.shape), _const_spec(kf.shape),
                  _const_spec(cd.shape),
                  _const_spec(w_out_blocks.shape)],
        out_specs=pl.BlockSpec((1, tile, d), fwd),
        out_shape=jax.ShapeDtypeStruct((b, s, d), F32),
        scratch_shapes=[pltpu.VMEM((N_RET_HEADS, LANES, LANES), F32),
                        pltpu.VMEM((tile, w_out_blocks.shape[1]), BF16),
                        pltpu.VMEM((pool_blks, POOL_HALO, LANES), BF16)],
        compiler_params=pltpu.CompilerParams(
            dimension_semantics=("arbitrary", "arbitrary"),
            vmem_limit_bytes=VMEM_LIMIT_BYTES),
        name="mixer",
    )(x, mod, g_post, acts, acts, band, pool_w_blocks, dmask, qf, qb, kf, cd, w_out_blocks)


def _layer(xs, cs_list, ada_w, ada_b, g_pre, g_post, w_in, pool_w, pool_scale, dec_f, dec_b, w_out,
           rope):
    d = xs[0].shape[2]
    d_pool = pool_scale.shape[0]
    d_mix = w_out.shape[0]
    d_ret = d_mix - d_pool
    assert d_ret == N_RET_HEADS * LANES and d_pool % (N_POOL_GROUPS * LANES) == 0
    assert w_in.shape[1] == d_pool + 3 * d_ret + d_mix

    dmask, qf, qb, kf, kb, cd = _decay_tables(dec_f, dec_b)
    mod = _adaln(jnp.concatenate(cs_list, axis=0), ada_w, ada_b)
    w_in_b = _col_blocks(w_in.astype(BF16))
    w_out_blocks = _col_blocks(w_out.astype(BF16))
    pool_w_blocks = _col_blocks(pool_w.astype(BF16))
    band = _col_blocks(_band_matrices())
    g_pre2 = g_pre.reshape(1, d)
    g_post2 = g_post.reshape(1, d)
    pscale2 = pool_scale.reshape(1, d_pool)

    outs = []
    row = 0
    for x in xs:
        b, s, _ = x.shape
        assert s % TOKEN_TILE == 0 and OUT_SPLITS[-1] * CHUNK == TOKEN_TILE
        m = mod[row:row + b].reshape(b, 3, d)
        row += b
        acts = _inproj(x, m, g_pre2, w_in_b, rope, kb, cd, pscale2, d_pool, d_ret, d_mix)
        outs.append(_mixer(x, m, g_post2, acts, band, pool_w_blocks, dmask, qf, qb, kf, cd,
                           w_out_blocks, d_pool, d_ret))
    return outs


def kernel(x_prompt, x_sample, c_prompt, c_sample, ada_w, ada_b, norm_pre, norm_post, w_in, pool_w,
           pool_scale, ret_decay_fwd, ret_decay_bwd, w_out):
    xs = [x_prompt, x_sample]
    rope = _rope_tables(max(x.shape[1] for x in xs))
    for l in range(ada_w.shape[0]):
        xs = _layer(xs, [c_prompt, c_sample], ada_w[l], ada_b[l], norm_pre[l], norm_post[l],
                    w_in[l], pool_w[l], pool_scale[l], ret_decay_fwd[l], ret_decay_bwd[l],
                    w_out[l], rope)
    return tuple(xs)
```

```python
import functools

import numpy as np
import jax
import jax.numpy as jnp
from jax import lax
from jax.experimental import pallas as pl
from jax.experimental.pallas import tpu as pltpu

F32 = jnp.float32
BF16 = jnp.bfloat16

POOL_WINDOWS = (2, 4, 8, 16)
N_POOL_GROUPS = len(POOL_WINDOWS)
N_RET_HEADS = 8
CHUNK = 128
POOL_ROWS = 128
ROPE_BASE = 10000.0
ROPE_BLOCK = 128
NORM_EPS = 1e-6
LANES = 128
POOL_HALO = 64
TOKEN_TILE = 512
PROJ_PIECE = 512
OUT_SPLITS = (4,)
ITEM_LAG = 2
VMEM_LIMIT_BYTES = 56 * 1024 * 1024


def _const_spec(shape):
    zeros = (0,) * len(shape)
    return pl.BlockSpec(shape, lambda *_: zeros, pipeline_mode=pl.Buffered(1))


def _sigmoid(x):
    return 1.0 / (1.0 + jnp.exp(-x))


def _col_blocks(x):
    *lead, r, c = x.shape
    return jnp.swapaxes(x.reshape(*lead, r, c // LANES, LANES), -2, -3)


def _split_acts(acts_ref, d_pool, d_ret):
    edges = np.cumsum([0, d_pool, d_ret, d_ret, d_ret, d_pool + d_ret]) // LANES
    views = [acts_ref.at[:, int(lo):int(hi)] for lo, hi in zip(edges[:-1], edges[1:])]
    return views + [acts_ref.at[:, int(edges[-1]):]]


def _cat_blocks(ref, idx, blocks, rows=slice(None)):
    return jnp.concatenate([ref[idx + (blk, rows, slice(None))] for blk in blocks], axis=1)


def _rope_table_kernel(freq_ref, out_ref):
    n_blocks = out_ref.shape[0] // ROPE_BLOCK
    freq = freq_ref[...]
    lo = lax.broadcasted_iota(jnp.int32, (ROPE_BLOCK, LANES), 0).astype(F32) * freq
    hi = (lax.broadcasted_iota(jnp.int32, (n_blocks, LANES), 0) * ROPE_BLOCK).astype(F32) * freq
    cos_lo, sin_lo = jnp.cos(lo), jnp.sin(lo)
    cos_hi, sin_hi = jnp.cos(hi), jnp.sin(hi)
    lane = lax.broadcasted_iota(jnp.int32, (ROPE_BLOCK, LANES), 1)
    sign = jnp.where(lane < LANES // 2, -1.0, 1.0)
    for blk in range(n_blocks):
        rows = slice(blk * ROPE_BLOCK, (blk + 1) * ROPE_BLOCK)
        ch, sh = cos_hi[blk:blk + 1, :], sin_hi[blk:blk + 1, :]
        out_ref[rows, :LANES] = ch * cos_lo - sh * sin_lo
        out_ref[rows, LANES:] = (sh * cos_lo + ch * sin_lo) * sign


def _rope_tables(seq_len):
    half = LANES // 2
    freqs = ROPE_BASE ** (-jnp.arange(half, dtype=F32) / half)
    freq_row = jnp.concatenate([freqs, freqs]).reshape(1, LANES)
    assert seq_len % ROPE_BLOCK == 0
    return pl.pallas_call(
        _rope_table_kernel,
        out_shape=jax.ShapeDtypeStruct((seq_len, 2 * LANES), F32),
        compiler_params=pltpu.CompilerParams(vmem_limit_bytes=VMEM_LIMIT_BYTES),
        name="rope_tables",
    )(freq_row)


def _decay_table_kernel(af_ref, ab_ref, dmask_ref, qf_ref, qb_ref, kf_ref, kb_ref, cd_ref):
    lgf = jnp.log1p(-jnp.exp2(-af_ref[...]))
    lgb = jnp.log1p(-jnp.exp2(-ab_ref[...]))
    width = lgf.shape[1]
    idx = lax.broadcasted_iota(jnp.int32, (CHUNK, width), 0).astype(F32)
    qf_ref[...] = jnp.exp((idx + 1.0) * lgf)
    kf_ref[...] = jnp.exp((CHUNK - 1.0 - idx) * lgf)
    qb_ref[...] = jnp.exp((CHUNK - idx) * lgb)
    kb_ref[...] = jnp.exp(idx * lgb)
    row = lax.broadcasted_iota(jnp.int32, (8, width), 0)
    cd_ref[...] = jnp.where(row == 0, jnp.exp(CHUNK * lgf),
                            jnp.where(row == 1, jnp.exp(CHUNK * lgb), 1.0))
    ii = lax.broadcasted_iota(jnp.int32, (CHUNK, CHUNK), 0)
    jj = lax.broadcasted_iota(jnp.int32, (CHUNK, CHUNK), 1)
    diff = (ii - jj).astype(F32)
    for h in range(N_RET_HEADS):
        cols = slice(h * LANES, (h + 1) * LANES)
        head_f = jnp.concatenate([lgf[:, cols]] * (CHUNK // LANES), axis=1)
        head_b = jnp.concatenate([lgb[:, cols]] * (CHUNK // LANES), axis=1)
        fwd = jnp.exp(jnp.maximum(diff, 0.0) * head_f)
        bwd = jnp.exp(jnp.maximum(-diff, 0.0) * head_b)
        dmask_ref[h] = jnp.where(diff >= 0, fwd, bwd)


def _decay_tables(dec_f, dec_b):
    width = N_RET_HEADS * LANES
    af = jnp.repeat(dec_f.astype(F32), LANES).reshape(1, width)
    ab = jnp.repeat(dec_b.astype(F32), LANES).reshape(1, width)
    tab = jax.ShapeDtypeStruct((CHUNK, width), F32)
    return pl.pallas_call(
        _decay_table_kernel,
        out_shape=[jax.ShapeDtypeStruct((N_RET_HEADS, CHUNK, CHUNK), F32), tab, tab, tab, tab,
                   jax.ShapeDtypeStruct((8, width), F32)],
        name="decay_tables",
    )(af, ab)


def _adaln_kernel(c_ref, w_ref, b_ref, o_ref):
    c = c_ref[...]
    act = c * _sigmoid(c)
    o_ref[...] = jnp.dot(act, w_ref[...], preferred_element_type=F32,
                         precision=lax.Precision.HIGHEST) + b_ref[...]


def _adaln(c, ada_w, ada_b):
    n, d = c.shape
    e = ada_w.shape[1]
    blk = 1024
    return pl.pallas_call(
        _adaln_kernel,
        grid=(e // blk,),
        in_specs=[pl.BlockSpec((n, d), lambda i: (0, 0)),
                  pl.BlockSpec((d, blk), lambda i: (0, i)),
                  pl.BlockSpec((1, blk), lambda i: (0, i))],
        out_specs=pl.BlockSpec((n, blk), lambda i: (0, i)),
        out_shape=jax.ShapeDtypeStruct((n, e), F32),
        name="adaln",
    )(c, ada_w, ada_b.reshape(1, e))


def _inproj_kernel(x_ref, mod_ref, gpre_ref, w_ref, rope_ref, kb_ref, cd_ref, pscale_ref,
                   acts_ref, state_ref, *, d_pool, d_ret):
    u_ref, q_ref, k_ref, v_ref, g_ref, sb_ref = _split_acts(acts_ref, d_pool, d_ret)

    @pl.when(pl.program_id(1) == 0)
    def _():
        state_ref[...] = jnp.zeros_like(state_ref)

    tile = x_ref.shape[1]
    x = x_ref[0]
    ms = jnp.mean(x * x, axis=-1, keepdims=True)
    gain = gpre_ref[...] * (1.0 + mod_ref[0, 1:2, :])
    hb = (x * lax.rsqrt(ms + NORM_EPS) * gain + mod_ref[0, 0:1, :]).astype(BF16)

    def proj(lo, width, row_parts=1):
        w = _cat_blocks(w_ref, (), range(lo // LANES, (lo + width) // LANES))
        step = tile // row_parts
        outs = [jnp.dot(hb[r:r + step], w, preferred_element_type=F32) for r in range(0, tile, step)]
        return outs[0] if row_parts == 1 else jnp.concatenate(outs, axis=0)

    def store_blocks(ref, lo, value):
        for i in range(PROJ_PIECE // LANES):
            ref[0, lo // LANES + i] = value[:, i * LANES:(i + 1) * LANES]

    def gate_piece(lo):
        z = proj(d_pool + 3 * d_ret + lo, PROJ_PIECE, row_parts=2 if lo == 0 else 1)
        half_z = 0.5 * z
        gate = half_z * (1.0 + jnp.tanh(half_z))
        if lo < d_pool:
            gate = gate * pscale_ref[:, lo:lo + PROJ_PIECE]
        store_blocks(g_ref, lo, gate.astype(BF16))

    pieces = range(0, d_ret, PROJ_PIECE)
    k_parts, v_parts = [], []
    gate_los = iter(range(0, g_ref.shape[1] * LANES, PROJ_PIECE))
    for lo in pieces:
        gate_piece(next(gate_los))
        k_parts.append(proj(d_pool + d_ret + lo, PROJ_PIECE))
    for lo in pieces:
        gate_piece(next(gate_los))
        vp = proj(d_pool + 2 * d_ret + lo, PROJ_PIECE).astype(BF16)
        store_blocks(v_ref, lo, vp)
        v_parts.append(vp)

    cs = rope_ref[:, :LANES]
    sn = rope_ref[:, LANES:]

    def rotary(t):
        return t * cs + pltpu.roll(t, LANES // 2, 1) * sn

    k_scale = float(LANES) ** -0.5
    n_chunks = tile // CHUNK
    heads_per_piece = PROJ_PIECE // LANES
    for hd in range(N_RET_HEADS):
        cols = slice(hd * LANES, (hd + 1) * LANES)
        part, local = hd // heads_per_piece, hd % heads_per_piece
        pcols = slice(local * LANES, (local + 1) * LANES)
        kh = rotary(k_parts[part][:, pcols]) * k_scale
        k_ref[0, hd] = kh.astype(BF16)
        vh = v_parts[part][:, pcols]
        for c in reversed(range(n_chunks)):
            rows = slice(c * CHUNK, (c + 1) * CHUNK)
            st = state_ref[hd]
            sb_ref[0, hd, rows, :] = st.astype(BF16)
            kd = (kh[rows] * kb_ref[:, cols]).astype(BF16)
            upd = lax.dot_general(kd, vh[rows], (((0,), (0,)), ((), ())),
                                  preferred_element_type=F32)
            state_ref[hd] = st * cd_ref[1:2, cols] + upd

    for lo in pieces:
        qp = proj(d_pool + lo, PROJ_PIECE)
        for i in range(heads_per_piece):
            q_ref[0, lo // LANES + i] = rotary(qp[:, i * LANES:(i + 1) * LANES]).astype(BF16)
    for lo in range(0, d_pool, PROJ_PIECE):
        store_blocks(u_ref, lo, proj(lo, PROJ_PIECE).astype(BF16))


def _inproj(x, mod, g_pre, w_in_bf16, rope, kb, cd, pool_scale, d_pool, d_ret, d_mix):
    b, s, d = x.shape
    tile = TOKEN_TILE
    n_tiles = s // tile
    rev = lambda bi, j: (bi, n_tiles - 1 - j, 0)
    act_blocks = (d_pool + 3 * d_ret + d_mix) // LANES + N_RET_HEADS
    return pl.pallas_call(
        functools.partial(_inproj_kernel, d_pool=d_pool, d_ret=d_ret),
        grid=(b, n_tiles),
        in_specs=[pl.BlockSpec((1, tile, d), rev),
                  pl.BlockSpec((1, 3, d), lambda bi, j: (bi, 0, 0)),
                  _const_spec((1, d)),
                  _const_spec(w_in_bf16.shape),
                  pl.BlockSpec((tile, 2 * LANES), lambda bi, j: (n_tiles - 1 - j, 0)),
                  _const_spec(kb.shape),
                  _const_spec(cd.shape),
                  _const_spec((1, d_pool))],
        out_specs=pl.BlockSpec((1, act_blocks, tile, LANES),
                               lambda bi, j: (bi, 0, n_tiles - 1 - j, 0)),
        out_shape=jax.ShapeDtypeStruct((b, act_blocks, s, LANES), BF16),
        scratch_shapes=[pltpu.VMEM((N_RET_HEADS, LANES, LANES), F32)],
        compiler_params=pltpu.CompilerParams(
            dimension_semantics=("arbitrary", "arbitrary"),
            vmem_limit_bytes=VMEM_LIMIT_BYTES),
        name="inproj",
    )(x, mod, g_pre, w_in_bf16, rope, kb, cd, pool_scale)


def _mixer_kernel(x_ref, mod_ref, gpost_ref, acts_ref, unext_ref, band_ref,
                  poolw_ref, dmask_ref, qf_ref, qb_ref, kf_ref, cd_ref, wout_ref, y_ref,
                  state_ref, ybuf_ref, uprev_ref, *, seq_len, d_pool, d_ret):
    u_ref, q_ref, k_ref, v_ref, g_ref, sb_ref = _split_acts(acts_ref, d_pool, d_ret)
    j = pl.program_id(1)
    n_tiles = pl.num_programs(1)
    tile = x_ref.shape[1]
    pool_group = d_pool // N_POOL_GROUPS
    group_blocks = pool_group // LANES

    @pl.when(j == 0)
    def _():
        state_ref[...] = jnp.zeros_like(state_ref)
        uprev_ref[...] = jnp.zeros_like(uprev_ref)

    def pool_window(sub, blk):
        lo, hi = sub * POOL_ROWS - POOL_HALO, (sub + 1) * POOL_ROWS + POOL_HALO
        parts = []
        if lo < 0:
            parts.append(uprev_ref[blk])
        parts.append(u_ref[0, blk, max(lo, 0):min(hi, tile), :])
        if hi > tile:
            nxt = unext_ref[0, blk]
            parts.append(jnp.where(j == n_tiles - 1, jnp.zeros_like(nxt), nxt))
        return jnp.concatenate(parts, axis=0) if len(parts) > 1 else parts[0]

    head_cols = [slice(hd * LANES, (hd + 1) * LANES) for hd in range(N_RET_HEADS)]
    pool_cols = [slice(g * pool_group, (g + 1) * pool_group) for g in range(N_POOL_GROUPS)]
    pool_blocks = [range(g * group_blocks, (g + 1) * group_blocks) for g in range(N_POOL_GROUPS)]

    items = [(c, hd) for c in range(tile // CHUNK) for hd in range(N_RET_HEADS)]
    units = [(sub, g) for sub in range(tile // POOL_ROWS) for g in range(N_POOL_GROUPS)]
    assert len(items) % len(units) == 0 or len(units) % len(items) == 0
    item_units = {it: [u for n, u in enumerate(units) if n * len(items) // len(units) == i]
                  for i, it in enumerate(items)}

    def front(c, hd):
        rows = slice(c * CHUNK, (c + 1) * CHUNK)
        cols = head_cols[hd]
        q = q_ref[0, hd, rows, :]
        k = k_ref[0, hd, rows, :]
        v = v_ref[0, hd, rows, :]
        st = state_ref[hd]
        vals = dict(q=q, v=v, st=st)
        vals["scores"] = lax.dot_general(q, k, (((1,), (1,)), ((), ())), preferred_element_type=F32)
        vals["kd_t"] = (k.astype(F32) * kf_ref[:, cols]).astype(BF16).T
        vals["wsums"] = []
        for sub, g in item_units[(c, hd)]:
            win = jnp.concatenate([pool_window(sub, blk) for blk in pool_blocks[g]], axis=1)
            band = _cat_blocks(band_ref, (g,), range(band_ref.shape[1]))
            vals["wsums"].append(jnp.dot(band, win, preferred_element_type=F32))
        return vals

    def back(c, hd, vals):
        rows = slice(c * CHUNK, (c + 1) * CHUNK)
        cols = head_cols[hd]
        inner = (vals["scores"] * dmask_ref[hd]).astype(BF16)
        stacked = jnp.dot(jnp.concatenate([inner, vals["kd_t"]], axis=0), vals["v"],
                          preferred_element_type=F32)
        o = stacked[:CHUNK]
        st = vals["st"]
        state_ref[hd] = st * cd_ref[0:1, cols] + stacked[CHUNK:]
        states = jnp.concatenate([st.astype(BF16), sb_ref[0, hd, rows, :]], axis=1)
        cross = jnp.dot(vals["q"], states, preferred_element_type=F32)
        o = o + qf_ref[:, cols] * cross[:, :LANES] + qb_ref[:, cols] * cross[:, LANES:]
        mu = jnp.mean(o, axis=-1, keepdims=True)
        dev = o - mu
        var = jnp.mean(dev * dev, axis=-1, keepdims=True)
        on = dev * lax.rsqrt(var + NORM_EPS)
        zcols = slice(d_pool + hd * LANES, d_pool + (hd + 1) * LANES)
        gate = g_ref[0, d_pool // LANES + hd, rows, :].astype(F32)
        ybuf_ref[rows, zcols] = (on * gate).astype(BF16)
        for (sub, g), wsum in zip(item_units[(c, hd)], vals["wsums"]):
            w = POOL_WINDOWS[g]
            prows = slice(sub * POOL_ROWS, (sub + 1) * POOL_ROWS)
            pos = lax.broadcasted_iota(jnp.int32, (POOL_ROWS, 1), 0) + (j * tile + sub * POOL_ROWS)
            cnt = jnp.minimum(pos + w // 2, seq_len) - jnp.maximum(pos - w // 2, 0)
            centre = _cat_blocks(u_ref, (0,), pool_blocks[g], prows).astype(F32)
            p = wsum / cnt.astype(F32) - centre
            pool_w = _cat_blocks(poolw_ref, (g,), range(poolw_ref.shape[1]))
            yp = jnp.dot(p.astype(BF16), pool_w, preferred_element_type=F32)
            pgate = _cat_blocks(g_ref, (0,), pool_blocks[g], prows).astype(F32)
            ybuf_ref[prows, pool_cols[g]] = (yp * pgate).astype(BF16)

    def output_rows(lo, hi):
        w_out = _cat_blocks(wout_ref, (), range(wout_ref.shape[0]))
        out = jnp.dot(ybuf_ref[lo:hi, :], w_out, preferred_element_type=F32)
        ms = jnp.mean(out * out, axis=-1, keepdims=True)
        gain = gpost_ref[...] * mod_ref[0, 2:3, :]
        y_ref[0, lo:hi, :] = x_ref[0, lo:hi, :] + out * lax.rsqrt(ms + NORM_EPS) * gain

    pending = {}
    for step in range(len(items) + ITEM_LAG):
        if step < len(items):
            pending[step] = front(*items[step])
        done = step - ITEM_LAG
        if done >= 0:
            c, hd = items[done]
            back(c, hd, pending.pop(done))
            if hd == N_RET_HEADS - 1 and c + 1 in OUT_SPLITS:
                prev = ([0] + [s for s in OUT_SPLITS if s < c + 1])[-1]
                output_rows(prev * CHUNK, (c + 1) * CHUNK)

    uprev_ref[...] = u_ref[0, :, tile - POOL_HALO:tile, :]


def _band_matrices():
    i = np.arange(POOL_ROWS)[:, None] + POOL_HALO
    m = np.arange(POOL_ROWS + 2 * POOL_HALO)[None, :]
    mats = [((m >= i - w // 2) & (m < i + w // 2)) for w in POOL_WINDOWS]
    return jnp.asarray(np.stack(mats).astype(np.float32), dtype=BF16)


def _mixer(x, mod, g_post, acts, band, pool_w_blocks, dmask, qf, qb, kf, cd, w_out_blocks,
           d_pool, d_ret):
    b, s, d = x.shape
    tile = TOKEN_TILE
    n_tiles = s // tile
    pool_blks = d_pool // LANES
    halo_per_tile = tile // POOL_HALO
    n_halo_blocks = s // POOL_HALO
    fwd = lambda bi, j: (bi, j, 0)
    next_map = lambda bi, j: (bi, 0, jnp.minimum((j + 1) * halo_per_tile, n_halo_blocks - 1), 0)
    return pl.pallas_call(
        functools.partial(_mixer_kernel, seq_len=s, d_pool=d_pool, d_ret=d_ret),
        grid=(b, n_tiles),
        in_specs=[pl.BlockSpec((1, tile, d), fwd),
                  pl.BlockSpec((1, 3, d), lambda bi, j: (bi, 0, 0)),
                  _const_spec((1, d)),
                  pl.BlockSpec((1, acts.shape[1], tile, LANES), lambda bi, j: (bi, 0, j, 0)),
                  pl.BlockSpec((1, pool_blks, POOL_HALO, LANES), next_map),
                  _const_spec(band.shape),
                  _const_spec(pool_w_blocks.shape),
                  _const_spec(dmask.shape),
                  _const_spec(qf.shape), _const_spec(qb.shape), _const_spec(kf.shape),
                  _const_spec(cd.shape),
                  _const_spec(w_out_blocks.shape)],
        out_specs=pl.BlockSpec((1, tile, d), fwd),
        out_shape=jax.ShapeDtypeStruct((b, s, d), F32),
        scratch_shapes=[pltpu.VMEM((N_RET_HEADS, LANES, LANES), F32),
                        pltpu.VMEM((tile, w_out_blocks.shape[1]), BF16),
                        pltpu.VMEM((pool_blks, POOL_HALO, LANES), BF16)],
        compiler_params=pltpu.CompilerParams(
            dimension_semantics=("arbitrary", "arbitrary"),
            vmem_limit_bytes=VMEM_LIMIT_BYTES),
        name="mixer",
    )(x, mod, g_post, acts, acts, band, pool_w_blocks, dmask, qf, qb, kf, cd, w_out_blocks)


def _layer(xs, cs_list, ada_w, ada_b, g_pre, g_post, w_in, pool_w, pool_scale, dec_f, dec_b, w_out,
           rope):
    d = xs[0].shape[2]
    d_pool = pool_scale.shape[0]
    d_mix = w_out.shape[0]
    d_ret = d_mix - d_pool
    assert d_ret == N_RET_HEADS * LANES and d_pool % (N_POOL_GROUPS * LANES) == 0
    assert w_in.shape[1] == d_pool + 3 * d_ret + d_mix

    dmask, qf, qb, kf, kb, cd = _decay_tables(dec_f, dec_b)
    mod = _adaln(jnp.concatenate(cs_list, axis=0), ada_w, ada_b)
    w_in_b = _col_blocks(w_in.astype(BF16))
    w_out_blocks = _col_blocks(w_out.astype(BF16))
    pool_w_blocks = _col_blocks(pool_w.astype(BF16))
    band = _col_blocks(_band_matrices())
    g_pre2 = g_pre.reshape(1, d)
    g_post2 = g_post.reshape(1, d)
    pscale2 = pool_scale.reshape(1, d_pool)

    outs = []
    row = 0
    for x in xs:
        b, s, _ = x.shape
        assert s % TOKEN_TILE == 0 and OUT_SPLITS[-1] * CHUNK == TOKEN_TILE
        m = mod[row:row + b].reshape(b, 3, d)
        row += b
        acts = _inproj(x, m, g_pre2, w_in_b, rope, kb, cd, pscale2, d_pool, d_ret, d_mix)
        outs.append(_mixer(x, m, g_post2, acts, band, pool_w_blocks, dmask, qf, qb, kf, cd,
                           w_out_blocks, d_pool, d_ret))
    return outs


def kernel(x_prompt, x_sample, c_prompt, c_sample, ada_w, ada_b, norm_pre, norm_post, w_in, pool_w,
           pool_scale, ret_decay_fwd, ret_decay_bwd, w_out):
    xs = [x_prompt, x_sample]
    rope = _rope_tables(max(x.shape[1] for x in xs))
    for l in range(ada_w.shape[0]):
        xs = _layer(xs, [c_prompt, c_sample], ada_w[l], ada_b[l], norm_pre[l], norm_post[l],
                    w_in[l], pool_w[l], pool_scale[l], ret_decay_fwd[l], ret_decay_bwd[l],
                    w_out[l], rope)
    return tuple(xs)
```

```python
import functools

import numpy as np
import jax
import jax.numpy as jnp
from jax import lax
from jax.experimental import pallas as pl
from jax.experimental.pallas import tpu as pltpu

F32 = jnp.float32
BF16 = jnp.bfloat16

POOL_WINDOWS = (2, 4, 8, 16)
N_POOL_GROUPS = len(POOL_WINDOWS)
N_RET_HEADS = 8
CHUNK = 128
POOL_ROWS = 128
ROPE_BASE = 10000.0
ROPE_BLOCK = 128
NORM_EPS = 1e-6
LANES = 128
POOL_HALO = 64
TOKEN_TILE = 512
PROJ_PIECE = 512
OUT_SPLITS = (4,)
ITEM_LAG = 2
VMEM_LIMIT_BYTES = 56 * 1024 * 1024


def _const_spec(shape):
    zeros = (0,) * len(shape)
    return pl.BlockSpec(shape, lambda *_: zeros, pipeline_mode=pl.Buffered(1))


def _sigmoid(x):
    return 1.0 / (1.0 + jnp.exp(-x))


def _col_blocks(x):
    *lead, r, c = x.shape
    return jnp.swapaxes(x.reshape(*lead, r, c // LANES, LANES), -2, -3)


def _split_acts(acts_ref, d_pool, d_ret):
    edges = np.cumsum([0, d_pool, d_ret, d_ret, d_ret, d_pool + d_ret]) // LANES
    views = [acts_ref.at[:, int(lo):int(hi)] for lo, hi in zip(edges[:-1], edges[1:])]
    return views + [acts_ref.at[:, int(edges[-1]):]]


def _cat_blocks(ref, idx, blocks, rows=slice(None)):
    return jnp.concatenate([ref[idx + (blk, rows, slice(None))] for blk in blocks], axis=1)


def _rope_table_kernel(freq_ref, out_ref):
    n_blocks = out_ref.shape[0] // ROPE_BLOCK
    freq = freq_ref[...]
    lo = lax.broadcasted_iota(jnp.int32, (ROPE_BLOCK, LANES), 0).astype(F32) * freq
    hi = (lax.broadcasted_iota(jnp.int32, (n_blocks, LANES), 0) * ROPE_BLOCK).astype(F32) * freq
    cos_lo, sin_lo = jnp.cos(lo), jnp.sin(lo)
    cos_hi, sin_hi = jnp.cos(hi), jnp.sin(hi)
    lane = lax.broadcasted_iota(jnp.int32, (ROPE_BLOCK, LANES), 1)
    sign = jnp.where(lane < LANES // 2, -1.0, 1.0)
    for blk in range(n_blocks):
        rows = slice(blk * ROPE_BLOCK, (blk + 1) * ROPE_BLOCK)
        ch, sh = cos_hi[blk:blk + 1, :], sin_hi[blk:blk + 1, :]
        out_ref[rows, :LANES] = ch * cos_lo - sh * sin_lo
        out_ref[rows, LANES:] = (sh * cos_lo + ch * sin_lo) * sign


def _rope_tables(seq_len):
    half = LANES // 2
    freqs = ROPE_BASE ** (-jnp.arange(half, dtype=F32) / half)
    freq_row = jnp.concatenate([freqs, freqs]).reshape(1, LANES)
    assert seq_len % ROPE_BLOCK == 0
    return pl.pallas_call(
        _rope_table_kernel,
        out_shape=jax.ShapeDtypeStruct((seq_len, 2 * LANES), F32),
        compiler_params=pltpu.CompilerParams(vmem_limit_bytes=VMEM_LIMIT_BYTES),
        name="rope_tables",
    )(freq_row)


def _decay_table_kernel(af_ref, ab_ref, dmask_ref, qf_ref, qb_ref, kf_ref, kb_ref, cd_ref):
    lgf = jnp.log1p(-jnp.exp2(-af_ref[...]))
    lgb = jnp.log1p(-jnp.exp2(-ab_ref[...]))
    width = lgf.shape[1]
    idx = lax.broadcasted_iota(jnp.int32, (CHUNK, width), 0).astype(F32)
    qf_ref[...] = jnp.exp((idx + 1.0) * lgf)
    kf_ref[...] = jnp.exp((CHUNK - 1.0 - idx) * lgf)
    qb_ref[...] = jnp.exp((CHUNK - idx) * lgb)
    kb_ref[...] = jnp.exp(idx * lgb)
    row = lax.broadcasted_iota(jnp.int32, (8, width), 0)
    cd_ref[...] = jnp.where(row == 0, jnp.exp(CHUNK * lgf),
                            jnp.where(row == 1, jnp.exp(CHUNK * lgb), 1.0))
    ii = lax.broadcasted_iota(jnp.int32, (CHUNK, CHUNK), 0)
    jj = lax.broadcasted_iota(jnp.int32, (CHUNK, CHUNK), 1)
    diff = (ii - jj).astype(F32)
    for h in range(N_RET_HEADS):
        cols = slice(h * LANES, (h + 1) * LANES)
        head_f = jnp.concatenate([lgf[:, cols]] * (CHUNK // LANES), axis=1)
        head_b = jnp.concatenate([lgb[:, cols]] * (CHUNK // LANES), axis=1)
        fwd = jnp.exp(jnp.maximum(diff, 0.0) * head_f)
        bwd = jnp.exp(jnp.maximum(-diff, 0.0) * head_b)
        dmask_ref[h] = jnp.where(diff >= 0, fwd, bwd)


def _decay_tables(dec_f, dec_b):
    width = N_RET_HEADS * LANES
    af = jnp.repeat(dec_f.astype(F32), LANES).reshape(1, width)
    ab = jnp.repeat(dec_b.astype(F32), LANES).reshape(1, width)
    tab = jax.ShapeDtypeStruct((CHUNK, width), F32)
    return pl.pallas_call(
        _decay_table_kernel,
        out_shape=[jax.ShapeDtypeStruct((N_RET_HEADS, CHUNK, CHUNK), F32), tab, tab, tab, tab,
                   jax.ShapeDtypeStruct((8, width), F32)],
        name="decay_tables",
    )(af, ab)


def _adaln_kernel(c_ref, w_ref, b_ref, o_ref):
    c = c_ref[...]
    act = c * _sigmoid(c)
    w = w_ref[...]

    def split(t):
        hi = t.astype(BF16)
        return hi, (t - hi.astype(F32)).astype(BF16)

    a_hi, a_lo = split(act)
    w_hi, w_lo = split(w)
    dot = functools.partial(jnp.dot, preferred_element_type=F32)
    o_ref[...] = dot(a_hi, w_hi) + (dot(a_hi, w_lo) + dot(a_lo, w_hi)) + b_ref[...]


def _adaln(c, ada_w, ada_b):
    n, d = c.shape
    e = ada_w.shape[1]
    blk = 1024
    return pl.pallas_call(
        _adaln_kernel,
        grid=(e // blk,),
        in_specs=[pl.BlockSpec((n, d), lambda i: (0, 0)),
                  pl.BlockSpec((d, blk), lambda i: (0, i)),
                  pl.BlockSpec((1, blk), lambda i: (0, i))],
        out_specs=pl.BlockSpec((n, blk), lambda i: (0, i)),
        out_shape=jax.ShapeDtypeStruct((n, e), F32),
        name="adaln",
    )(c, ada_w, ada_b.reshape(1, e))


def _inproj_kernel(x_ref, mod_ref, gpre_ref, w_ref, rope_ref, kb_ref, cd_ref, pscale_ref,
                   acts_ref, state_ref, *, d_pool, d_ret):
    u_ref, q_ref, k_ref, v_ref, g_ref, sb_ref = _split_acts(acts_ref, d_pool, d_ret)

    @pl.when(pl.program_id(1) == 0)
    def _():
        state_ref[...] = jnp.zeros_like(state_ref)

    tile = x_ref.shape[1]
    x = x_ref[0]
    ms = jnp.mean(x * x, axis=-1, keepdims=True)
    gain = gpre_ref[...] * (1.0 + mod_ref[0, 1:2, :])
    hb = (x * lax.rsqrt(ms + NORM_EPS) * gain + mod_ref[0, 0:1, :]).astype(BF16)

    def proj(lo, width, row_parts=1):
        w = _cat_blocks(w_ref, (), range(lo // LANES, (lo + width) // LANES))
        step = tile // row_parts
        outs = [jnp.dot(hb[r:r + step], w, preferred_element_type=F32) for r in range(0, tile, step)]
        return outs[0] if row_parts == 1 else jnp.concatenate(outs, axis=0)

    def store_blocks(ref, lo, value):
        for i in range(PROJ_PIECE // LANES):
            ref[0, lo // LANES + i] = value[:, i * LANES:(i + 1) * LANES]

    def gate_piece(lo):
        z = proj(d_pool + 3 * d_ret + lo, PROJ_PIECE, row_parts=2 if lo == 0 else 1)
        half_z = 0.5 * z
        gate = half_z * (1.0 + jnp.tanh(half_z))
        if lo < d_pool:
            gate = gate * pscale_ref[:, lo:lo + PROJ_PIECE]
        store_blocks(g_ref, lo, gate.astype(BF16))

    pieces = range(0, d_ret, PROJ_PIECE)
    k_parts, v_parts = [], []
    gate_los = iter(range(0, g_ref.shape[1] * LANES, PROJ_PIECE))
    for lo in pieces:
        gate_piece(next(gate_los))
        k_parts.append(proj(d_pool + d_ret + lo, PROJ_PIECE))
    for lo in pieces:
        gate_piece(next(gate_los))
        vp = proj(d_pool + 2 * d_ret + lo, PROJ_PIECE).astype(BF16)
        store_blocks(v_ref, lo, vp)
        v_parts.append(vp)

    cs = rope_ref[:, :LANES]
    sn = rope_ref[:, LANES:]

    def rotary(t):
        return t * cs + pltpu.roll(t, LANES // 2, 1) * sn

    k_scale = float(LANES) ** -0.5
    n_chunks = tile // CHUNK
    heads_per_piece = PROJ_PIECE // LANES
    for hd in range(N_RET_HEADS):
        cols = slice(hd * LANES, (hd + 1) * LANES)
        part, local = hd // heads_per_piece, hd % heads_per_piece
        pcols = slice(local * LANES, (local + 1) * LANES)
        kh = rotary(k_parts[part][:, pcols]) * k_scale
        k_ref[0, hd] = kh.astype(BF16)
        vh = v_parts[part][:, pcols]
        for c in reversed(range(n_chunks)):
            rows = slice(c * CHUNK, (c + 1) * CHUNK)
            st = state_ref[hd]
            sb_ref[0, hd, rows, :] = st.astype(BF16)
            kd = (kh[rows] * kb_ref[:, cols]).astype(BF16)
            upd = lax.dot_general(kd, vh[rows], (((0,), (0,)), ((), ())),
                                  preferred_element_type=F32)
            state_ref[hd] = st * cd_ref[1:2, cols] + upd

    for lo in pieces:
        qp = proj(d_pool + lo, PROJ_PIECE)
        for i in range(heads_per_piece):
            q_ref[0, lo // LANES + i] = rotary(qp[:, i * LANES:(i + 1) * LANES]).astype(BF16)
    for lo in range(0, d_pool, PROJ_PIECE):
        store_blocks(u_ref, lo, proj(lo, PROJ_PIECE).astype(BF16))


def _inproj(x, mod, g_pre, w_in_bf16, rope, kb, cd, pool_scale, d_pool, d_ret, d_mix):
    b, s, d = x.shape
    tile = TOKEN_TILE
    n_tiles = s // tile
    rev = lambda bi, j: (bi, n_tiles - 1 - j, 0)
    act_blocks = (d_pool + 3 * d_ret + d_mix) // LANES + N_RET_HEADS
    return pl.pallas_call(
        functools.partial(_inproj_kernel, d_pool=d_pool, d_ret=d_ret),
        grid=(b, n_tiles),
        in_specs=[pl.BlockSpec((1, tile, d), rev),
                  pl.BlockSpec((1, 3, d), lambda bi, j: (bi, 0, 0)),
                  _const_spec((1, d)),
                  _const_spec(w_in_bf16.shape),
                  pl.BlockSpec((tile, 2 * LANES), lambda bi, j: (n_tiles - 1 - j, 0)),
                  _const_spec(kb.shape),
                  _const_spec(cd.shape),
                  _const_spec((1, d_pool))],
        out_specs=pl.BlockSpec((1, act_blocks, tile, LANES),
                               lambda bi, j: (bi, 0, n_tiles - 1 - j, 0)),
        out_shape=jax.ShapeDtypeStruct((b, act_blocks, s, LANES), BF16),
        scratch_shapes=[pltpu.VMEM((N_RET_HEADS, LANES, LANES), F32)],
        compiler_params=pltpu.CompilerParams(
            dimension_semantics=("arbitrary", "arbitrary"),
            vmem_limit_bytes=VMEM_LIMIT_BYTES),
        name="inproj",
    )(x, mod, g_pre, w_in_bf16, rope, kb, cd, pool_scale)


def _mixer_kernel(x_ref, mod_ref, gpost_ref, acts_ref, unext_ref, band_ref,
                  poolw_ref, dmask_ref, qf_ref, qb_ref, kf_ref, cd_ref, wout_ref, y_ref,
                  state_ref, ybuf_ref, uprev_ref, *, seq_len, d_pool, d_ret):
    u_ref, q_ref, k_ref, v_ref, g_ref, sb_ref = _split_acts(acts_ref, d_pool, d_ret)
    j = pl.program_id(1)
    n_tiles = pl.num_programs(1)
    tile = x_ref.shape[1]
    pool_group = d_pool // N_POOL_GROUPS
    group_blocks = pool_group // LANES

    @pl.when(j == 0)
    def _():
        state_ref[...] = jnp.zeros_like(state_ref)
        uprev_ref[...] = jnp.zeros_like(uprev_ref)

    def pool_window(sub, blk):
        lo, hi = sub * POOL_ROWS - POOL_HALO, (sub + 1) * POOL_ROWS + POOL_HALO
        parts = []
        if lo < 0:
            parts.append(uprev_ref[blk])
        parts.append(u_ref[0, blk, max(lo, 0):min(hi, tile), :])
        if hi > tile:
            nxt = unext_ref[0, blk]
            parts.append(jnp.where(j == n_tiles - 1, jnp.zeros_like(nxt), nxt))
        return jnp.concatenate(parts, axis=0) if len(parts) > 1 else parts[0]

    head_cols = [slice(hd * LANES, (hd + 1) * LANES) for hd in range(N_RET_HEADS)]
    pool_cols = [slice(g * pool_group, (g + 1) * pool_group) for g in range(N_POOL_GROUPS)]
    pool_blocks = [range(g * group_blocks, (g + 1) * group_blocks) for g in range(N_POOL_GROUPS)]

    items = [(c, hd) for c in range(tile // CHUNK) for hd in range(N_RET_HEADS)]
    units = [(sub, g) for sub in range(tile // POOL_ROWS) for g in range(N_POOL_GROUPS)]
    assert len(items) % len(units) == 0 or len(units) % len(items) == 0
    item_units = {it: [u for n, u in enumerate(units) if n * len(items) // len(units) == i]
                  for i, it in enumerate(items)}

    def front(c, hd):
        rows = slice(c * CHUNK, (c + 1) * CHUNK)
        cols = head_cols[hd]
        q = q_ref[0, hd, rows, :]
        k = k_ref[0, hd, rows, :]
        v = v_ref[0, hd, rows, :]
        st = state_ref[hd]
        vals = dict(q=q, v=v, st=st)
        vals["scores"] = lax.dot_general(q, k, (((1,), (1,)), ((), ())), preferred_element_type=F32)
        vals["kd_t"] = (k.astype(F32) * kf_ref[:, cols]).astype(BF16).T
        vals["wsums"] = []
        for sub, g in item_units[(c, hd)]:
            win = jnp.concatenate([pool_window(sub, blk) for blk in pool_blocks[g]], axis=1)
            band = _cat_blocks(band_ref, (g,), range(band_ref.shape[1]))
            vals["wsums"].append(jnp.dot(band, win, preferred_element_type=F32))
        return vals

    def back(c, hd, vals):
        rows = slice(c * CHUNK, (c + 1) * CHUNK)
        cols = head_cols[hd]
        inner = (vals["scores"] * dmask_ref[hd]).astype(BF16)
        stacked = jnp.dot(jnp.concatenate([inner, vals["kd_t"]], axis=0), vals["v"],
                          preferred_element_type=F32)
        o = stacked[:CHUNK]
        st = vals["st"]
        state_ref[hd] = st * cd_ref[0:1, cols] + stacked[CHUNK:]
        states = jnp.concatenate([st.astype(BF16), sb_ref[0, hd, rows, :]], axis=1)
        cross = jnp.dot(vals["q"], states, preferred_element_type=F32)
        o = o + qf_ref[:, cols] * cross[:, :LANES] + qb_ref[:, cols] * cross[:, LANES:]
        mu = jnp.mean(o, axis=-1, keepdims=True)
        dev = o - mu
        var = jnp.mean(dev * dev, axis=-1, keepdims=True)
        on = dev * lax.rsqrt(var + NORM_EPS)
        zcols = slice(d_pool + hd * LANES, d_pool + (hd + 1) * LANES)
        gate = g_ref[0, d_pool // LANES + hd, rows, :].astype(F32)
        ybuf_ref[rows, zcols] = (on * gate).astype(BF16)
        for (sub, g), wsum in zip(item_units[(c, hd)], vals["wsums"]):
            w = POOL_WINDOWS[g]
            prows = slice(sub * POOL_ROWS, (sub + 1) * POOL_ROWS)
            pos = lax.broadcasted_iota(jnp.int32, (POOL_ROWS, 1), 0) + (j * tile + sub * POOL_ROWS)
            cnt = jnp.minimum(pos + w // 2, seq_len) - jnp.maximum(pos - w // 2, 0)
            centre = _cat_blocks(u_ref, (0,), pool_blocks[g], prows).astype(F32)
            p = wsum / cnt.astype(F32) - centre
            pool_w = _cat_blocks(poolw_ref, (g,), range(poolw_ref.shape[1]))
            yp = jnp.dot(p.astype(BF16), pool_w, preferred_element_type=F32)
            pgate = _cat_blocks(g_ref, (0,), pool_blocks[g], prows).astype(F32)
            ybuf_ref[prows, pool_cols[g]] = (yp * pgate).astype(BF16)

    def output_rows(lo, hi):
        w_out = _cat_blocks(wout_ref, (), range(wout_ref.shape[0]))
        out = jnp.dot(ybuf_ref[lo:hi, :], w_out, preferred_element_type=F32)
        ms = jnp.mean(out * out, axis=-1, keepdims=True)
        gain = gpost_ref[...] * mod_ref[0, 2:3, :]
        y_ref[0, lo:hi, :] = x_ref[0, lo:hi, :] + out * lax.rsqrt(ms + NORM_EPS) * gain

    pending = {}
    for step in range(len(items) + ITEM_LAG):
        if step < len(items):
            pending[step] = front(*items[step])
        done = step - ITEM_LAG
        if done >= 0:
            c, hd = items[done]
            back(c, hd, pending.pop(done))
            if hd == N_RET_HEADS - 1 and c + 1 in OUT_SPLITS:
                prev = ([0] + [s for s in OUT_SPLITS if s < c + 1])[-1]
                output_rows(prev * CHUNK, (c + 1) * CHUNK)

    uprev_ref[...] = u_ref[0, :, tile - POOL_HALO:tile, :]


def _band_matrices():
    i = np.arange(POOL_ROWS)[:, None] + POOL_HALO
    m = np.arange(POOL_ROWS + 2 * POOL_HALO)[None, :]
    mats = [((m >= i - w // 2) & (m < i + w // 2)) for w in POOL_WINDOWS]
    return jnp.asarray(np.stack(mats).astype(np.float32), dtype=BF16)


def _mixer(x, mod, g_post, acts, band, pool_w_blocks, dmask, qf, qb, kf, cd, w_out_blocks,
           d_pool, d_ret):
    b, s, d = x.shape
    tile = TOKEN_TILE
    n_tiles = s // tile
    pool_blks = d_pool // LANES
    halo_per_tile = tile // POOL_HALO
    n_halo_blocks = s // POOL_HALO
    fwd = lambda bi, j: (bi, j, 0)
    next_map = lambda bi, j: (bi, 0, jnp.minimum((j + 1) * halo_per_tile, n_halo_blocks - 1), 0)
    return pl.pallas_call(
        functools.partial(_mixer_kernel, seq_len=s, d_pool=d_pool, d_ret=d_ret),
        grid=(b, n_tiles),
        in_specs=[pl.BlockSpec((1, tile, d), fwd),
                  pl.BlockSpec((1, 3, d), lambda bi, j: (bi, 0, 0)),
                  _const_spec((1, d)),
                  pl.BlockSpec((1, acts.shape[1], tile, LANES), lambda bi, j: (bi, 0, j, 0)),
                  pl.BlockSpec((1, pool_blks, POOL_HALO, LANES), next_map),
                  _const_spec(band.shape),
                  _const_spec(pool_w_blocks.shape),
                  _const_spec(dmask.shape),
                  _const_spec(qf.shape), _const_spec(qb.shape), _const_spec(kf.shape),
                  _const_spec(cd.shape),
                  _const_spec(w_out_blocks.shape)],
        out_specs=pl.BlockSpec((1, tile, d), fwd),
        out_shape=jax.ShapeDtypeStruct((b, s, d), F32),
        scratch_shapes=[pltpu.VMEM((N_RET_HEADS, LANES, LANES), F32),
                        pltpu.VMEM((tile, w_out_blocks.shape[1]), BF16),
                        pltpu.VMEM((pool_blks, POOL_HALO, LANES), BF16)],
        compiler_params=pltpu.CompilerParams(
            dimension_semantics=("arbitrary", "arbitrary"),
            vmem_limit_bytes=VMEM_LIMIT_BYTES),
        name="mixer",
    )(x, mod, g_post, acts, acts, band, pool_w_blocks, dmask, qf, qb, kf, cd, w_out_blocks)


def _layer(xs, cs_list, ada_w, ada_b, g_pre, g_post, w_in, pool_w, pool_scale, dec_f, dec_b, w_out,
           rope):
    d = xs[0].shape[2]
    d_pool = pool_scale.shape[0]
    d_mix = w_out.shape[0]
    d_ret = d_mix - d_pool
    assert d_ret == N_RET_HEADS * LANES and d_pool % (N_POOL_GROUPS * LANES) == 0
    assert w_in.shape[1] == d_pool + 3 * d_ret + d_mix

    dmask, qf, qb, kf, kb, cd = _decay_tables(dec_f, dec_b)
    mod = _adaln(jnp.concatenate(cs_list, axis=0), ada_w, ada_b)
    w_in_b = _col_blocks(w_in.astype(BF16))
    w_out_blocks = _col_blocks(w_out.astype(BF16))
    pool_w_blocks = _col_blocks(pool_w.astype(BF16))
    band = _col_blocks(_band_matrices())
    g_pre2 = g_pre.reshape(1, d)
    g_post2 = g_post.reshape(1, d)
    pscale2 = pool_scale.reshape(1, d_pool)

    outs = []
    row = 0
    for x in xs:
        b, s, _ = x.shape
        assert s % TOKEN_TILE == 0 and OUT_SPLITS[-1] * CHUNK == TOKEN_TILE
        m = mod[row:row + b].reshape(b, 3, d)
        row += b
        acts = _inproj(x, m, g_pre2, w_in_b, rope, kb, cd, pscale2, d_pool, d_ret, d_mix)
        outs.append(_mixer(x, m, g_post2, acts, band, pool_w_blocks, dmask, qf, qb, kf, cd,
                           w_out_blocks, d_pool, d_ret))
    return outs


def kernel(x_prompt, x_sample, c_prompt, c_sample, ada_w, ada_b, norm_pre, norm_post, w_in, pool_w,
           pool_scale, ret_decay_fwd, ret_decay_bwd, w_out):
    xs = [x_prompt, x_sample]
    rope = _rope_tables(max(x.shape[1] for x in xs))
    for l in range(ada_w.shape[0]):
        xs = _layer(xs, [c_prompt, c_sample], ada_w[l], ada_b[l], norm_pre[l], norm_post[l],
                    w_in[l], pool_w[l], pool_scale[l], ret_decay_fwd[l], ret_decay_bwd[l],
                    w_out[l], rope)
    return tuple(xs)
```

```python
import functools

import numpy as np
import jax
import jax.numpy as jnp
from jax import lax
from jax.experimental import pallas as pl
from jax.experimental.pallas import tpu as pltpu

F32 = jnp.float32
BF16 = jnp.bfloat16

POOL_WINDOWS = (2, 4, 8, 16)
N_POOL_GROUPS = len(POOL_WINDOWS)
N_RET_HEADS = 8
CHUNK = 128
POOL_ROWS = 128
ROPE_BASE = 10000.0
ROPE_BLOCK = 128
NORM_EPS = 1e-6
LANES = 128
POOL_HALO = 64
TOKEN_TILE = 512
PROJ_PIECE = 512
OUT_SPLITS = (4,)
ITEM_LAG = 2
VMEM_LIMIT_BYTES = 56 * 1024 * 1024


def _const_spec(shape):
    zeros = (0,) * len(shape)
    return pl.BlockSpec(shape, lambda *_: zeros, pipeline_mode=pl.Buffered(1))


def _sigmoid(x):
    return 1.0 / (1.0 + jnp.exp(-x))


def _col_blocks(x):
    *lead, r, c = x.shape
    return jnp.swapaxes(x.reshape(*lead, r, c // LANES, LANES), -2, -3)


def _split_acts(acts_ref, d_pool, d_ret):
    edges = np.cumsum([0, d_pool, d_ret, d_ret, d_ret, d_pool + d_ret]) // LANES
    views = [acts_ref.at[:, int(lo):int(hi)] for lo, hi in zip(edges[:-1], edges[1:])]
    return views + [acts_ref.at[:, int(edges[-1]):]]


def _cat_blocks(ref, idx, blocks, rows=slice(None)):
    return jnp.concatenate([ref[idx + (blk, rows, slice(None))] for blk in blocks], axis=1)


def _rope_table_kernel(freq_ref, out_ref):
    n_blocks = out_ref.shape[0] // ROPE_BLOCK
    freq = freq_ref[...]
    lo = lax.broadcasted_iota(jnp.int32, (ROPE_BLOCK, LANES), 0).astype(F32) * freq
    hi = (lax.broadcasted_iota(jnp.int32, (n_blocks, LANES), 0) * ROPE_BLOCK).astype(F32) * freq
    cos_lo, sin_lo = jnp.cos(lo), jnp.sin(lo)
    cos_hi, sin_hi = jnp.cos(hi), jnp.sin(hi)
    lane = lax.broadcasted_iota(jnp.int32, (ROPE_BLOCK, LANES), 1)
    sign = jnp.where(lane < LANES // 2, -1.0, 1.0)
    for blk in range(n_blocks):
        rows = slice(blk * ROPE_BLOCK, (blk + 1) * ROPE_BLOCK)
        ch, sh = cos_hi[blk:blk + 1, :], sin_hi[blk:blk + 1, :]
        out_ref[rows, :LANES] = ch * cos_lo - sh * sin_lo
        out_ref[rows, LANES:] = (sh * cos_lo + ch * sin_lo) * sign


def _rope_tables(seq_len):
    half = LANES // 2
    freqs = ROPE_BASE ** (-jnp.arange(half, dtype=F32) / half)
    freq_row = jnp.concatenate([freqs, freqs]).reshape(1, LANES)
    assert seq_len % ROPE_BLOCK == 0
    return pl.pallas_call(
        _rope_table_kernel,
        out_shape=jax.ShapeDtypeStruct((seq_len, 2 * LANES), F32),
        compiler_params=pltpu.CompilerParams(vmem_limit_bytes=VMEM_LIMIT_BYTES),
        name="rope_tables",
    )(freq_row)


def _decay_table_kernel(af_ref, ab_ref, dmask_ref, qf_ref, qb_ref, kf_ref, kb_ref, cd_ref):
    lgf = jnp.log1p(-jnp.exp2(-af_ref[...]))
    lgb = jnp.log1p(-jnp.exp2(-ab_ref[...]))
    width = lgf.shape[1]
    idx = lax.broadcasted_iota(jnp.int32, (CHUNK, width), 0).astype(F32)
    qf_ref[...] = jnp.exp((idx + 1.0) * lgf)
    kf_ref[...] = jnp.exp((CHUNK - 1.0 - idx) * lgf)
    qb_ref[...] = jnp.exp((CHUNK - idx) * lgb)
    kb_ref[...] = jnp.exp(idx * lgb)
    row = lax.broadcasted_iota(jnp.int32, (8, width), 0)
    cd_ref[...] = jnp.where(row == 0, jnp.exp(CHUNK * lgf),
                            jnp.where(row == 1, jnp.exp(CHUNK * lgb), 1.0))
    ii = lax.broadcasted_iota(jnp.int32, (CHUNK, CHUNK), 0)
    jj = lax.broadcasted_iota(jnp.int32, (CHUNK, CHUNK), 1)
    diff = (ii - jj).astype(F32)
    for h in range(N_RET_HEADS):
        cols = slice(h * LANES, (h + 1) * LANES)
        head_f = jnp.concatenate([lgf[:, cols]] * (CHUNK // LANES), axis=1)
        head_b = jnp.concatenate([lgb[:, cols]] * (CHUNK // LANES), axis=1)
        fwd = jnp.exp(jnp.maximum(diff, 0.0) * head_f)
        bwd = jnp.exp(jnp.maximum(-diff, 0.0) * head_b)
        dmask_ref[h] = jnp.where(diff >= 0, fwd, bwd)


def _decay_tables(dec_f, dec_b):
    width = N_RET_HEADS * LANES
    af = jnp.repeat(dec_f.astype(F32), LANES).reshape(1, width)
    ab = jnp.repeat(dec_b.astype(F32), LANES).reshape(1, width)
    tab = jax.ShapeDtypeStruct((CHUNK, width), F32)
    return pl.pallas_call(
        _decay_table_kernel,
        out_shape=[jax.ShapeDtypeStruct((N_RET_HEADS, CHUNK, CHUNK), F32), tab, tab, tab, tab,
                   jax.ShapeDtypeStruct((8, width), F32)],
        name="decay_tables",
    )(af, ab)


def _adaln_kernel(c_ref, w_ref, b_ref, o_ref):
    c = c_ref[...]
    act = c * _sigmoid(c)
    o_ref[...] = jnp.dot(act, w_ref[...], preferred_element_type=F32,
                         precision=lax.Precision.HIGHEST) + b_ref[...]


def _adaln(c, ada_w, ada_b):
    n, d = c.shape
    e = ada_w.shape[1]
    blk = 1024
    return pl.pallas_call(
        _adaln_kernel,
        grid=(e // blk,),
        in_specs=[pl.BlockSpec((n, d), lambda i: (0, 0)),
                  pl.BlockSpec((d, blk), lambda i: (0, i)),
                  pl.BlockSpec((1, blk), lambda i: (0, i))],
        out_specs=pl.BlockSpec((n, blk), lambda i: (0, i)),
        out_shape=jax.ShapeDtypeStruct((n, e), F32),
        name="adaln",
    )(c, ada_w, ada_b.reshape(1, e))


def _inproj_kernel(x_ref, mod_ref, gpre_ref, w_ref, rope_ref, kb_ref, cd_ref, pscale_ref,
                   acts_ref, state_ref, *, d_pool, d_ret):
    u_ref, q_ref, k_ref, v_ref, g_ref, sb_ref = _split_acts(acts_ref, d_pool, d_ret)

    @pl.when(pl.program_id(1) == 0)
    def _():
        state_ref[...] = jnp.zeros_like(state_ref)

    tile = x_ref.shape[1]
    x = x_ref[0]
    ms = jnp.mean(x * x, axis=-1, keepdims=True)
    gain = gpre_ref[...] * (1.0 + mod_ref[0, 1:2, :])
    hb = (x * lax.rsqrt(ms + NORM_EPS) * gain + mod_ref[0, 0:1, :]).astype(BF16)

    def proj(lo, width, row_parts=1):
        w = _cat_blocks(w_ref, (), range(lo // LANES, (lo + width) // LANES))
        step = tile // row_parts
        outs = [jnp.dot(hb[r:r + step], w, preferred_element_type=F32) for r in range(0, tile, step)]
        return outs[0] if row_parts == 1 else jnp.concatenate(outs, axis=0)

    def store_blocks(ref, lo, value):
        for i in range(PROJ_PIECE // LANES):
            ref[0, lo // LANES + i] = value[:, i * LANES:(i + 1) * LANES]

    def gate_piece(lo):
        z = proj(d_pool + 3 * d_ret + lo, PROJ_PIECE, row_parts=2 if lo == 0 else 1)
        half_z = 0.5 * z
        gate = half_z * (1.0 + jnp.tanh(half_z))
        if lo < d_pool:
            gate = gate * pscale_ref[:, lo:lo + PROJ_PIECE]
        store_blocks(g_ref, lo, gate.astype(BF16))

    pieces = range(0, d_ret, PROJ_PIECE)
    k_parts, v_parts = [], []
    gate_los = iter(range(0, g_ref.shape[1] * LANES, PROJ_PIECE))
    for lo in pieces:
        gate_piece(next(gate_los))
        k_parts.append(proj(d_pool + d_ret + lo, PROJ_PIECE))
    for lo in pieces:
        gate_piece(next(gate_los))
        vp = proj(d_pool + 2 * d_ret + lo, PROJ_PIECE).astype(BF16)
        store_blocks(v_ref, lo, vp)
        v_parts.append(vp)

    cs = rope_ref[:, :LANES]
    sn = rope_ref[:, LANES:]

    def rotary(t):
        return t * cs + pltpu.roll(t, LANES // 2, 1) * sn

    k_scale = float(LANES) ** -0.5
    n_chunks = tile // CHUNK
    heads_per_piece = PROJ_PIECE // LANES
    for hd in range(N_RET_HEADS):
        cols = slice(hd * LANES, (hd + 1) * LANES)
        part, local = hd // heads_per_piece, hd % heads_per_piece
        pcols = slice(local * LANES, (local + 1) * LANES)
        kh = rotary(k_parts[part][:, pcols]) * k_scale
        k_ref[0, hd] = kh.astype(BF16)
        vh = v_parts[part][:, pcols]
        for c in reversed(range(n_chunks)):
            rows = slice(c * CHUNK, (c + 1) * CHUNK)
            st = state_ref[hd]
            sb_ref[0, hd, rows, :] = st.astype(BF16)
            kd = (kh[rows] * kb_ref[:, cols]).astype(BF16)
            upd = lax.dot_general(kd, vh[rows], (((0,), (0,)), ((), ())),
                                  preferred_element_type=F32)
            state_ref[hd] = st * cd_ref[1:2, cols] + upd

    for lo in pieces:
        qp = proj(d_pool + lo, PROJ_PIECE)
        for i in range(heads_per_piece):
            q_ref[0, lo // LANES + i] = rotary(qp[:, i * LANES:(i + 1) * LANES]).astype(BF16)
    for lo in range(0, d_pool, PROJ_PIECE):
        store_blocks(u_ref, lo, proj(lo, PROJ_PIECE).astype(BF16))


def _inproj(x, mod, g_pre, w_in_bf16, rope, kb, cd, pool_scale, d_pool, d_ret, d_mix):
    b, s, d = x.shape
    tile = TOKEN_TILE
    n_tiles = s // tile
    rev = lambda bi, j: (bi, n_tiles - 1 - j, 0)
    act_blocks = (d_pool + 3 * d_ret + d_mix) // LANES + N_RET_HEADS
    return pl.pallas_call(
        functools.partial(_inproj_kernel, d_pool=d_pool, d_ret=d_ret),
        grid=(b, n_tiles),
        in_specs=[pl.BlockSpec((1, tile, d), rev),
                  pl.BlockSpec((1, 3, d), lambda bi, j: (bi, 0, 0)),
                  _const_spec((1, d)),
                  _const_spec(w_in_bf16.shape),
                  pl.BlockSpec((tile, 2 * LANES), lambda bi, j: (n_tiles - 1 - j, 0)),
                  _const_spec(kb.shape),
                  _const_spec(cd.shape),
                  _const_spec((1, d_pool))],
        out_specs=pl.BlockSpec((1, act_blocks, tile, LANES),
                               lambda bi, j: (bi, 0, n_tiles - 1 - j, 0)),
        out_shape=jax.ShapeDtypeStruct((b, act_blocks, s, LANES), BF16),
        scratch_shapes=[pltpu.VMEM((N_RET_HEADS, LANES, LANES), F32)],
        compiler_params=pltpu.CompilerParams(
            dimension_semantics=("arbitrary", "arbitrary"),
            vmem_limit_bytes=VMEM_LIMIT_BYTES),
        name="inproj",
    )(x, mod, g_pre, w_in_bf16, rope, kb, cd, pool_scale)


def _mixer_kernel(x_ref, mod_ref, gpost_ref, acts_ref, unext_ref, band_ref,
                  poolw_ref, dmask_ref, qf_ref, qb_ref, kf_ref, cd_ref, wout_ref, y_ref,
                  state_ref, ybuf_ref, uprev_ref, *, seq_len, d_pool, d_ret):
    u_ref, q_ref, k_ref, v_ref, g_ref, sb_ref = _split_acts(acts_ref, d_pool, d_ret)
    j = pl.program_id(1)
    n_tiles = pl.num_programs(1)
    tile = x_ref.shape[1]
    pool_group = d_pool // N_POOL_GROUPS
    group_blocks = pool_group // LANES

    @pl.when(j == 0)
    def _():
        state_ref[...] = jnp.zeros_like(state_ref)
        uprev_ref[...] = jnp.zeros_like(uprev_ref)

    def pool_window(sub, blk):
        lo, hi = sub * POOL_ROWS - POOL_HALO, (sub + 1) * POOL_ROWS + POOL_HALO
        parts = []
        if lo < 0:
            parts.append(uprev_ref[blk])
        parts.append(u_ref[0, blk, max(lo, 0):min(hi, tile), :])
        if hi > tile:
            nxt = unext_ref[0, blk]
            parts.append(jnp.where(j == n_tiles - 1, jnp.zeros_like(nxt), nxt))
        return jnp.concatenate(parts, axis=0) if len(parts) > 1 else parts[0]

    head_cols = [slice(hd * LANES, (hd + 1) * LANES) for hd in range(N_RET_HEADS)]
    pool_cols = [slice(g * pool_group, (g + 1) * pool_group) for g in range(N_POOL_GROUPS)]
    pool_blocks = [range(g * group_blocks, (g + 1) * group_blocks) for g in range(N_POOL_GROUPS)]

    items = [(c, hd) for c in range(tile // CHUNK) for hd in range(N_RET_HEADS)]
    units = [(sub, g) for sub in range(tile // POOL_ROWS) for g in range(N_POOL_GROUPS)]
    assert len(items) % len(units) == 0 or len(units) % len(items) == 0
    item_units = {it: [u for n, u in enumerate(units) if n * len(items) // len(units) == i]
                  for i, it in enumerate(items)}

    def front(c, hd):
        rows = slice(c * CHUNK, (c + 1) * CHUNK)
        cols = head_cols[hd]
        q = q_ref[0, hd, rows, :]
        k = k_ref[0, hd, rows, :]
        v = v_ref[0, hd, rows, :]
        st = state_ref[hd]
        vals = dict(q=q, v=v, st=st)
        vals["scores"] = lax.dot_general(q, k, (((1,), (1,)), ((), ())), preferred_element_type=F32)
        vals["kd_t"] = (k.astype(F32) * kf_ref[:, cols]).astype(BF16).T
        vals["wsums"] = []
        for sub, g in item_units[(c, hd)]:
            win = jnp.concatenate([pool_window(sub, blk) for blk in pool_blocks[g]], axis=1)
            band = _cat_blocks(band_ref, (g,), range(band_ref.shape[1]))
            vals["wsums"].append(jnp.dot(band, win, preferred_element_type=F32))
        return vals

    def back(c, hd, vals):
        rows = slice(c * CHUNK, (c + 1) * CHUNK)
        cols = head_cols[hd]
        inner = (vals["scores"] * dmask_ref[hd]).astype(BF16)
        stacked = jnp.dot(jnp.concatenate([inner, vals["kd_t"]], axis=0), vals["v"],
                          preferred_element_type=F32)
        o = stacked[:CHUNK]
        st = vals["st"]
        state_ref[hd] = st * cd_ref[0:1, cols] + stacked[CHUNK:]
        states = jnp.concatenate([st.astype(BF16), sb_ref[0, hd, rows, :]], axis=1)
        cross = jnp.dot(vals["q"], states, preferred_element_type=F32)
        o = o + qf_ref[:, cols] * cross[:, :LANES] + qb_ref[:, cols] * cross[:, LANES:]
        mu = jnp.mean(o, axis=-1, keepdims=True)
        dev = o - mu
        var = jnp.mean(dev * dev, axis=-1, keepdims=True)
        on = dev * lax.rsqrt(var + NORM_EPS)
        zcols = slice(d_pool + hd * LANES, d_pool + (hd + 1) * LANES)
        gate = g_ref[0, d_pool // LANES + hd, rows, :]
        ybuf_ref[rows, zcols] = on.astype(BF16) * gate
        for (sub, g), wsum in zip(item_units[(c, hd)], vals["wsums"]):
            w = POOL_WINDOWS[g]
            prows = slice(sub * POOL_ROWS, (sub + 1) * POOL_ROWS)
            pos = lax.broadcasted_iota(jnp.int32, (POOL_ROWS, 1), 0) + (j * tile + sub * POOL_ROWS)
            cnt = jnp.minimum(pos + w // 2, seq_len) - jnp.maximum(pos - w // 2, 0)
            centre = _cat_blocks(u_ref, (0,), pool_blocks[g], prows).astype(F32)
            p = wsum / cnt.astype(F32) - centre
            pool_w = _cat_blocks(poolw_ref, (g,), range(poolw_ref.shape[1]))
            yp = jnp.dot(p.astype(BF16), pool_w, preferred_element_type=F32)
            pgate = _cat_blocks(g_ref, (0,), pool_blocks[g], prows)
            ybuf_ref[prows, pool_cols[g]] = yp.astype(BF16) * pgate

    def output_rows(lo, hi):
        w_out = _cat_blocks(wout_ref, (), range(wout_ref.shape[0]))
        out = jnp.dot(ybuf_ref[lo:hi, :], w_out, preferred_element_type=F32)
        ms = jnp.mean(out * out, axis=-1, keepdims=True)
        gain = gpost_ref[...] * mod_ref[0, 2:3, :]
        y_ref[0, lo:hi, :] = x_ref[0, lo:hi, :] + out * lax.rsqrt(ms + NORM_EPS) * gain

    pending = {}
    for step in range(len(items) + ITEM_LAG):
        if step < len(items):
            pending[step] = front(*items[step])
        done = step - ITEM_LAG
        if done >= 0:
            c, hd = items[done]
            back(c, hd, pending.pop(done))
            if hd == N_RET_HEADS - 1 and c + 1 in OUT_SPLITS:
                prev = ([0] + [s for s in OUT_SPLITS if s < c + 1])[-1]
                output_rows(prev * CHUNK, (c + 1) * CHUNK)

    uprev_ref[...] = u_ref[0, :, tile - POOL_HALO:tile, :]


def _band_matrices():
    i = np.arange(POOL_ROWS)[:, None] + POOL_HALO
    m = np.arange(POOL_ROWS + 2 * POOL_HALO)[None, :]
    mats = [((m >= i - w // 2) & (m < i + w // 2)) for w in POOL_WINDOWS]
    return jnp.asarray(np.stack(mats).astype(np.float32), dtype=BF16)


def _mixer(x, mod, g_post, acts, band, pool_w_blocks, dmask, qf, qb, kf, cd, w_out_blocks,
           d_pool, d_ret):
    b, s, d = x.shape
    tile = TOKEN_TILE
    n_tiles = s // tile
    pool_blks = d_pool // LANES
    halo_per_tile = tile // POOL_HALO
    n_halo_blocks = s // POOL_HALO
    fwd = lambda bi, j: (bi, j, 0)
    next_map = lambda bi, j: (bi, 0, jnp.minimum((j + 1) * halo_per_tile, n_halo_blocks - 1), 0)
    return pl.pallas_call(
        functools.partial(_mixer_kernel, seq_len=s, d_pool=d_pool, d_ret=d_ret),
        grid=(b, n_tiles),
        in_specs=[pl.BlockSpec((1, tile, d), fwd),
                  pl.BlockSpec((1, 3, d), lambda bi, j: (bi, 0, 0)),
                  _const_spec((1, d)),
                  pl.BlockSpec((1, acts.shape[1], tile, LANES), lambda bi, j: (bi, 0, j, 0)),
                  pl.BlockSpec((1, pool_blks, POOL_HALO, LANES), next_map),
                  _const_spec(band.shape),
                  _const_spec(pool_w_blocks.shape),
                  _const_spec(dmask.shape),
                  _const_spec(qf.shape), _const_spec(qb.shape), _const_spec(kf.shape),
                  _const_spec(cd.shape),
                  _const_spec(w_out_blocks.shape)],
        out_specs=pl.BlockSpec((1, tile, d), fwd),
        out_shape=jax.ShapeDtypeStruct((b, s, d), F32),
        scratch_shapes=[pltpu.VMEM((N_RET_HEADS, LANES, LANES), F32),
                        pltpu.VMEM((tile, w_out_blocks.shape[1]), BF16),
                        pltpu.VMEM((pool_blks, POOL_HALO, LANES), BF16)],
        compiler_params=pltpu.CompilerParams(
            dimension_semantics=("arbitrary", "arbitrary"),
            vmem_limit_bytes=VMEM_LIMIT_BYTES),
        name="mixer",
    )(x, mod, g_post, acts, acts, band, pool_w_blocks, dmask, qf, qb, kf, cd, w_out_blocks)


def _layer(xs, cs_list, ada_w, ada_b, g_pre, g_post, w_in, pool_w, pool_scale, dec_f, dec_b, w_out,
           rope):
    d = xs[0].shape[2]
    d_pool = pool_scale.shape[0]
    d_mix = w_out.shape[0]
    d_ret = d_mix - d_pool
    assert d_ret == N_RET_HEADS * LANES and d_pool % (N_POOL_GROUPS * LANES) == 0
    assert w_in.shape[1] == d_pool + 3 * d_ret + d_mix

    dmask, qf, qb, kf, kb, cd = _decay_tables(dec_f, dec_b)
    mod = _adaln(jnp.concatenate(cs_list, axis=0), ada_w, ada_b)
    w_in_b = _col_blocks(w_in.astype(BF16))
    w_out_blocks = _col_blocks(w_out.astype(BF16))
    pool_w_blocks = _col_blocks(pool_w.astype(BF16))
    band = _col_blocks(_band_matrices())
    g_pre2 = g_pre.reshape(1, d)
    g_post2 = g_post.reshape(1, d)
    pscale2 = pool_scale.reshape(1, d_pool)

    outs = []
    row = 0
    for x in xs:
        b, s, _ = x.shape
        assert s % TOKEN_TILE == 0 and OUT_SPLITS[-1] * CHUNK == TOKEN_TILE
        m = mod[row:row + b].reshape(b, 3, d)
        row += b
        acts = _inproj(x, m, g_pre2, w_in_b, rope, kb, cd, pscale2, d_pool, d_ret, d_mix)
        outs.append(_mixer(x, m, g_post2, acts, band, pool_w_blocks, dmask, qf, qb, kf, cd,
                           w_out_blocks, d_pool, d_ret))
    return outs


def kernel(x_prompt, x_sample, c_prompt, c_sample, ada_w, ada_b, norm_pre, norm_post, w_in, pool_w,
           pool_scale, ret_decay_fwd, ret_decay_bwd, w_out):
    xs = [x_prompt, x_sample]
    rope = _rope_tables(max(x.shape[1] for x in xs))
    for l in range(ada_w.shape[0]):
        xs = _layer(xs, [c_prompt, c_sample], ada_w[l], ada_b[l], norm_pre[l], norm_post[l],
                    w_in[l], pool_w[l], pool_scale[l], ret_decay_fwd[l], ret_decay_bwd[l],
                    w_out[l], rope)
    return tuple(xs)
```
